```python
import math, functools
import jax, jax.numpy as jnp
from jax import lax
import numpy as np

D_MODEL = 1024
BATCH = 4
SEQ = 4096
DEPTH = 2
DEC_BATCH = 32
DEC_SEQ = 4
PAST_LEN = 16384
PAGE_SIZE = 128

N_HEADS = 8
HEAD_DIM = 64
D_ATTN = N_HEADS * HEAD_DIM
MOBA_BLOCK = 256
MOBA_TOPK = 3
Q_CHUNK = 128
D_SSM = 512
SSM_GROUP = 16
N_GROUPS = D_SSM // SSM_GROUP
STATE_DIM = 64
DT_MIN = 1e-3
DT_MAX = 1e-1
PROJ_SIZES = (D_ATTN, D_ATTN, D_ATTN, D_ATTN, D_SSM, D_SSM, D_MODEL, D_MODEL)
D_IN = sum(PROJ_SIZES)
RMS_EPS = 1e-6

kernel_name = 'moba_s5_gated_hybrid_step'


def rmsnorm(x, g):
    xf = x.astype(jnp.float32)
    y = xf * lax.rsqrt(jnp.mean(xf * xf, axis=-1, keepdims=True) + RMS_EPS)
    return (y * g.astype(jnp.float32)).astype(x.dtype)


def split_proj(h, w_in):
    p = h @ w_in
    offs = [int(o) for o in np.cumsum(PROJ_SIZES)[:-1]]
    return jnp.split(p, offs, axis=-1)


def to_blocks(rows):
    b, t = rows.shape[:2]
    return rows.reshape(b, t // MOBA_BLOCK, MOBA_BLOCK, N_HEADS, HEAD_DIM)


def gather_blocks(t, idx):
    b, h = t.shape[0], t.shape[3]
    bi = jnp.arange(b)[:, None, None, None]
    hi = jnp.arange(h)[None, :, None, None]
    return t[bi, idx, :, hi]


def moba_attend(q, q_pos, kb, vb, kmean):
    nb = kb.shape[1]
    n_sel = min(MOBA_TOPK, nb)
    qblk = q_pos // MOBA_BLOCK
    gate = jnp.einsum('bhqd,bnhd->bhqn', q.astype(jnp.float32), kmean)
    fully_past = jnp.arange(nb, dtype=jnp.int32)[None, :] < qblk[:, None]
    gate = jnp.where(fully_past, gate, -jnp.inf)
    _, sel = lax.top_k(gate, n_sel)
    sel = sel.astype(jnp.int32)
    sel_ok = sel < qblk[:, None]
    own = jnp.broadcast_to(qblk[:, None], sel.shape[:-1] + (1,))
    idx = jnp.concatenate([sel, own], axis=-1)
    blk_ok = jnp.concatenate([sel_ok, jnp.ones(own.shape, dtype=bool)], axis=-1)
    kg = gather_blocks(kb, idx)
    vg = gather_blocks(vb, idx)
    key_pos = idx[..., None] * MOBA_BLOCK + jnp.arange(MOBA_BLOCK, dtype=jnp.int32)
    ok = blk_ok[..., None] & (key_pos <= q_pos[:, None, None])
    logits = jnp.einsum('bhqd,bhqnkd->bhqnk', q, kg).astype(jnp.float32) * (HEAD_DIM ** -0.5)
    logits = jnp.where(ok, logits, -jnp.inf)
    b, h, nq = q.shape[:3]
    p = jax.nn.softmax(logits.reshape(b, h, nq, -1), axis=-1).reshape(logits.shape)
    return jnp.einsum('bhqnk,bhqnkd->bhqd', p.astype(vg.dtype), vg)


def moba_prompt(q, k, v):
    b, l = q.shape[:2]
    pad = (-l) % MOBA_BLOCK
    zp = jnp.zeros((b, pad, N_HEADS, HEAD_DIM), k.dtype)
    kb = to_blocks(jnp.concatenate([k, zp], axis=1))
    vb = to_blocks(jnp.concatenate([v, zp], axis=1))
    kmean = jnp.mean(kb.astype(jnp.float32), axis=2)
    n_chunks = l // Q_CHUNK
    qc = q.reshape(b, n_chunks, Q_CHUNK, N_HEADS, HEAD_DIM).transpose(1, 0, 3, 2, 4)

    def one_chunk(args):
        qi, c = args
        pos = c * Q_CHUNK + jnp.arange(Q_CHUNK, dtype=jnp.int32)
        return moba_attend(qi, pos, kb, vb, kmean)

    o = lax.map(one_chunk, (qc, jnp.arange(n_chunks, dtype=jnp.int32)))
    return o.transpose(1, 0, 3, 2, 4).reshape(b, l, D_ATTN)


def moba_sample(q, k, v, past_k, past_v):
    bd, s = q.shape[:2]
    past = past_k.shape[1] * PAGE_SIZE
    pad = (-(past + s)) % MOBA_BLOCK
    zp = jnp.zeros((bd, pad, N_HEADS, HEAD_DIM), k.dtype)
    kfull = jnp.concatenate([past_k.reshape(bd, past, N_HEADS, HEAD_DIM).astype(k.dtype), k, zp], axis=1)
    vfull = jnp.concatenate([past_v.reshape(bd, past, N_HEADS, HEAD_DIM).astype(v.dtype), v, zp], axis=1)
    kb, vb = to_blocks(kfull), to_blocks(vfull)
    kmean = jnp.mean(kb.astype(jnp.float32), axis=2)
    pos = past + jnp.arange(s, dtype=jnp.int32)
    o = moba_attend(q.transpose(0, 2, 1, 3), pos, kb, vb, kmean)
    return o.transpose(0, 2, 1, 3).reshape(bd, s, D_ATTN)


def s5_branch(u, h0_re, h0_im, lam_re, lam_im, log_dt, b_re, b_im, c_re, c_im, d_skip, w_glu):
    bt, l = u.shape[:2]
    f32 = jnp.float32
    uf = u.astype(f32).reshape(bt, l, N_GROUPS, SSM_GROUP)
    dt = jnp.exp(log_dt.astype(f32))[:, None]
    lr, li = lam_re.astype(f32), lam_im.astype(f32)
    mag = jnp.exp(lr * dt)
    a_re, a_im = mag * jnp.cos(li * dt), mag * jnp.sin(li * dt)
    den = lr * lr + li * li
    f_re = ((a_re - 1.0) * lr + a_im * li) / den
    f_im = (a_im * lr - (a_re - 1.0) * li) / den
    br, bi = b_re.astype(f32), b_im.astype(f32)
    bb_re = f_re[..., None] * br - f_im[..., None] * bi
    bb_im = f_re[..., None] * bi + f_im[..., None] * br
    bu_re = jnp.einsum('gph,blgh->blgp', bb_re, uf)
    bu_im = jnp.einsum('gph,blgh->blgp', bb_im, uf)
    h0r, h0i = h0_re.astype(f32), h0_im.astype(f32)
    bu_re = bu_re.at[:, 0].add(a_re * h0r - a_im * h0i)
    bu_im = bu_im.at[:, 0].add(a_re * h0i + a_im * h0r)
    ar = jnp.broadcast_to(a_re, bu_re.shape)
    ai = jnp.broadcast_to(a_im, bu_re.shape)

    def combine(e1, e2):
        a1r, a1i, b1r, b1i = e1
        a2r, a2i, b2r, b2i = e2
        return (a1r * a2r - a1i * a2i, a1r * a2i + a1i * a2r,
                a2r * b1r - a2i * b1i + b2r, a2r * b1i + a2i * b1r + b2i)

    _, _, hr, hi = lax.associative_scan(combine, (ar, ai, bu_re, bu_im), axis=1)
    y = (jnp.einsum('ghp,blgp->blgh', c_re.astype(f32), hr)
         - jnp.einsum('ghp,blgp->blgh', c_im.astype(f32), hi)
         + d_skip.astype(f32) * uf)
    y = jax.nn.gelu(y.reshape(bt, l, D_SSM))
    g = y.astype(w_glu.dtype) @ w_glu
    y = g[..., :D_SSM] * jax.nn.sigmoid(g[..., D_SSM:])
    return y.astype(u.dtype), hr[:, -1], hi[:, -1]


def hybrid_layer(x, attend, h0_re, h0_im, norm_g, w_in, lam_re, lam_im, log_dt, b_re, b_im,
                 c_re, c_im, d_skip, w_glu, w_pa, w_pb, w_out):
    bt, l = x.shape[:2]
    h = rmsnorm(x, norm_g)
    q, k, v, z_a, u_b, z_b, g_a, g_b = split_proj(h, w_in)
    q = q.reshape(bt, l, N_HEADS, HEAD_DIM)
    k = k.reshape(bt, l, N_HEADS, HEAD_DIM)
    v = v.reshape(bt, l, N_HEADS, HEAD_DIM)
    o_a = attend(q, k, v) * jax.nn.silu(z_a)
    y_b, h_re, h_im = s5_branch(u_b, h0_re, h0_im, lam_re, lam_im, log_dt, b_re, b_im,
                                c_re, c_im, d_skip, w_glu)
    y_b = y_b * jax.nn.silu(z_b)
    merged = jax.nn.sigmoid(g_a) * (o_a @ w_pa) + jax.nn.sigmoid(g_b) * (y_b @ w_pb)
    return x + merged @ w_out, k, v, h_re, h_im


def setup_inputs(seed: int = 0) -> dict:
    key = jax.random.key(seed)
    ks = jax.random.split(key, 24)
    f32 = jnp.float32
    nrm = lambda k, shape, s: s * jax.random.normal(k, shape, f32)
    n_pages = PAST_LEN // PAGE_SIZE
    n_used = DEC_BATCH * n_pages
    n_pool = n_used + max(1, n_used // 4)
    x_prompt = nrm(ks[0], (BATCH, SEQ, D_MODEL), 1.0)
    x_sample = nrm(ks[1], (DEC_BATCH, DEC_SEQ, D_MODEL), 1.0)
    cache_k = nrm(ks[2], (DEPTH, n_pool, PAGE_SIZE, N_HEADS, HEAD_DIM), 1.0)
    cache_v = nrm(ks[3], (DEPTH, n_pool, PAGE_SIZE, N_HEADS, HEAD_DIM), 1.0)
    state_ssm_re = nrm(ks[4], (DEPTH, DEC_BATCH, N_GROUPS, STATE_DIM), 0.5)
    state_ssm_im = nrm(ks[5], (DEPTH, DEC_BATCH, N_GROUPS, STATE_DIM), 0.5)
    page_table = jax.random.permutation(ks[6], n_pool)[:n_used].reshape(DEC_BATCH, n_pages).astype(jnp.int32)
    norm_g = 1.0 + nrm(ks[7], (DEPTH, D_MODEL), 0.01)
    w_in = nrm(ks[8], (DEPTH, D_MODEL, D_IN), D_MODEL ** -0.5)
    n_idx = jnp.arange(STATE_DIM, dtype=f32)
    lam_re = -0.5 + nrm(ks[9], (DEPTH, N_GROUPS, STATE_DIM), 0.01)
    lam_im = math.pi * n_idx + nrm(ks[10], (DEPTH, N_GROUPS, STATE_DIM), 0.01)
    log_dt = jax.random.uniform(ks[11], (DEPTH, N_GROUPS), f32, math.log(DT_MIN), math.log(DT_MAX))
    b_re = nrm(ks[12], (DEPTH, N_GROUPS, STATE_DIM, SSM_GROUP), (2 * SSM_GROUP) ** -0.5)
    b_im = nrm(ks[13], (DEPTH, N_GROUPS, STATE_DIM, SSM_GROUP), (2 * SSM_GROUP) ** -0.5)
    c_re = nrm(ks[14], (DEPTH, N_GROUPS, SSM_GROUP, STATE_DIM), STATE_DIM ** -0.5)
    c_im = nrm(ks[15], (DEPTH, N_GROUPS, SSM_GROUP, STATE_DIM), STATE_DIM ** -0.5)
    d_skip = nrm(ks[16], (DEPTH, N_GROUPS, SSM_GROUP), 1.0)
    w_glu = nrm(ks[17], (DEPTH, D_SSM, 2 * D_SSM), D_SSM ** -0.5)
    w_pa = nrm(ks[18], (DEPTH, D_ATTN, D_MODEL), D_ATTN ** -0.5)
    w_pb = nrm(ks[19], (DEPTH, D_SSM, D_MODEL), D_SSM ** -0.5)
    w_out = nrm(ks[20], (DEPTH, D_MODEL, D_MODEL), D_MODEL ** -0.5)
    final_norm_g = 1.0 + nrm(ks[21], (D_MODEL,), 0.01)
    return {'x_prompt': x_prompt, 'x_sample': x_sample, 'cache_k': cache_k, 'cache_v': cache_v,
            'state_ssm_re': state_ssm_re, 'state_ssm_im': state_ssm_im, 'page_table': page_table,
            'norm_g': norm_g, 'w_in': w_in, 'lam_re': lam_re, 'lam_im': lam_im, 'log_dt': log_dt,
            'b_re': b_re, 'b_im': b_im, 'c_re': c_re, 'c_im': c_im, 'd_skip': d_skip,
            'w_glu': w_glu, 'w_pa': w_pa, 'w_pb': w_pb, 'w_out': w_out, 'final_norm_g': final_norm_g}


def reference(x_prompt, x_sample, cache_k, cache_v, state_ssm_re, state_ssm_im, page_table,
              norm_g, w_in, lam_re, lam_im, log_dt, b_re, b_im, c_re, c_im, d_skip,
              w_glu, w_pa, w_pb, w_out, final_norm_g):
    xp, xs = x_prompt, x_sample
    h0 = jnp.zeros((x_prompt.shape[0], N_GROUPS, STATE_DIM), jnp.float32)
    kp_l, vp_l, ks_l, vs_l = [], [], [], []
    hpr_l, hpi_l, hsr_l, hsi_l = [], [], [], []
    for l in range(DEPTH):
        lw = (norm_g[l], w_in[l], lam_re[l], lam_im[l], log_dt[l], b_re[l], b_im[l],
              c_re[l], c_im[l], d_skip[l], w_glu[l], w_pa[l], w_pb[l], w_out[l])
        xp, kp, vp, hpr, hpi = hybrid_layer(xp, moba_prompt, h0, h0, *lw)
        attend_s = functools.partial(moba_sample, past_k=cache_k[l, page_table], past_v=cache_v[l, page_table])
        xs, ksn, vsn, hsr, hsi = hybrid_layer(xs, attend_s, state_ssm_re[l], state_ssm_im[l], *lw)
        kp_l.append(kp); vp_l.append(vp); ks_l.append(ksn); vs_l.append(vsn)
        hpr_l.append(hpr); hpi_l.append(hpi); hsr_l.append(hsr); hsi_l.append(hsi)
    y_prompt = rmsnorm(xp, final_norm_g)
    y_sample = rmsnorm(xs, final_norm_g)
    return (y_prompt, y_sample,
            jnp.stack(kp_l), jnp.stack(vp_l), jnp.stack(ks_l), jnp.stack(vs_l),
            jnp.stack(hpr_l), jnp.stack(hpi_l), jnp.stack(hsr_l), jnp.stack(hsi_l))
```

```python
import functools

import jax
import jax.numpy as jnp
from jax import lax
from jax.experimental import pallas as pl
from jax.experimental.pallas import tpu as pltpu

MOBA_BLOCK = 256
MOBA_TOPK = 3
RMS_EPS = 1e-6

V7X_LANES = 128
V7X_MXU_DIM = 256
V7X_VMEM_LIMIT_BYTES = 56 * 1024 * 1024

SSM_CHUNK = 16
MASKED_LOGIT = -1e30

_F32 = jnp.float32
_BF16 = jnp.bfloat16


def _cparams(*sem):
    return pltpu.CompilerParams(dimension_semantics=sem, vmem_limit_bytes=V7X_VMEM_LIMIT_BYTES)


def _rmsnorm(x, g):
    return x * lax.rsqrt(jnp.mean(x * x, axis=-1, keepdims=True) + RMS_EPS) * g


def _dot(a, b):
    return jnp.dot(a, b, preferred_element_type=_F32)


def _dot_nt(a, b):
    return lax.dot_general(a, b, (((1,), (1,)), ((), ())), preferred_element_type=_F32)


def _proj_kernel(x_ref, g_ref, w_ref, q_ref, k_ref, v_ref, kb_ref, vb_ref, u_ref, km_ref,
                 *, d_attn, d_ssm, head_dim, blk):
    hb = _rmsnorm(x_ref[...], g_ref[...]).astype(_BF16)
    q = _dot(hb, w_ref[:, 0:d_attn])
    k = _dot(hb, w_ref[:, d_attn:2 * d_attn])
    v = _dot(hb, w_ref[:, 2 * d_attn:3 * d_attn])
    u = _dot(hb, w_ref[:, 3 * d_attn:3 * d_attn + d_ssm])
    q_ref[...] = (q * (head_dim ** -0.5)).astype(_BF16)
    k_ref[...] = k
    v_ref[...] = v
    kb_ref[...] = k.astype(_BF16)
    vb_ref[...] = v.astype(_BF16)
    u_ref[...] = u
    if km_ref is not None:
        tm = k.shape[0]
        km_ref[0] = jnp.sum(k.reshape(tm // blk, blk, d_attn), axis=1) * (1.0 / blk)


def _proj_call(x2d, g, w_a, *, d_attn, d_ssm, head_dim, tm, with_kmean):
    t, d = x2d.shape
    n_tiles = t // tm
    row = lambda i: (i, 0)
    out_shape = [
        jax.ShapeDtypeStruct((t, d_attn), _BF16),
        jax.ShapeDtypeStruct((t, d_attn), _F32),
        jax.ShapeDtypeStruct((t, d_attn), _F32),
        jax.ShapeDtypeStruct((t, d_attn), _BF16),
        jax.ShapeDtypeStruct((t, d_attn), _BF16),
        jax.ShapeDtypeStruct((t, d_ssm), _F32),
    ]
    out_specs = [pl.BlockSpec((tm, d_attn), row)] * 5 + [pl.BlockSpec((tm, d_ssm), row)]
    body = functools.partial(_proj_kernel, d_attn=d_attn, d_ssm=d_ssm, head_dim=head_dim, blk=MOBA_BLOCK)
    if with_kmean:
        out_shape.append(jax.ShapeDtypeStruct((n_tiles, tm // MOBA_BLOCK, d_attn), _F32))
        out_specs.append(pl.BlockSpec((1, tm // MOBA_BLOCK, d_attn), lambda i: (i, 0, 0)))
        kern = body
    else:
        kern = lambda *refs: body(*refs, None)
    return pl.pallas_call(
        kern,
        grid=(n_tiles,),
        in_specs=[pl.BlockSpec((tm, d), row),
                  pl.BlockSpec((1, d), lambda i: (0, 0)),
                  pl.BlockSpec(w_a.shape, lambda i: (0, 0))],
        out_specs=out_specs,
        out_shape=out_shape,
        compiler_params=_cparams("parallel"),
        name="proj",
    )(x2d, g, w_a)


def _moba_prompt_kernel(q_ref, k_ref, v_ref, km_ref, o_ref, sel_ref, m_ref, l_ref, acc_ref,
                        *, nb, head_dim, blk):
    i = pl.program_id(2)
    width = q_ref.shape[-1]
    nh = width // head_dim
    rows = nh * blk

    q = q_ref[...]
    lane_head = lax.broadcasted_iota(jnp.int32, (blk, width), 1) // head_dim
    q4 = jnp.concatenate([jnp.where(lane_head == h, q, jnp.zeros_like(q)) for h in range(nh)], axis=0)

    gate = _dot_nt(km_ref[...].astype(_BF16), q4)
    n_iota = lax.broadcasted_iota(jnp.int32, (nb, rows), 0)
    past = n_iota < i
    gate = jnp.where(past, gate, -jnp.inf)
    rank = jnp.zeros((nb, rows), jnp.int32)
    for m in range(nb):
        gm = gate[m:m + 1, :]
        tie_first = jnp.where(m < n_iota, 1, 0)
        rank = rank + jnp.where(gm > gate, 1, jnp.where(gm == gate, tie_first, 0))
    sel = jnp.where(past & (rank < MOBA_TOPK), 1.0, 0.0).astype(_F32)
    sel_pad = jnp.concatenate([sel, jnp.zeros((V7X_LANES - nb, rows), _F32)], axis=0)
    sel_ref[...] = sel_pad.T

    start = pl.multiple_of(i * blk, blk)
    s = _dot_nt(q4, k_ref[pl.ds(start, blk), :])
    q_off = lax.broadcasted_iota(jnp.int32, (rows, blk), 0) % blk
    k_off = lax.broadcasted_iota(jnp.int32, (rows, blk), 1)
    s = jnp.where(k_off <= q_off, s, MASKED_LOGIT)
    m0 = jnp.max(s, axis=1, keepdims=True)
    p = jnp.exp(s - m0)
    m_ref[...] = m0
    l_ref[...] = jnp.sum(p, axis=1, keepdims=True)
    acc_ref[...] = _dot(p.astype(_BF16), v_ref[pl.ds(start, blk), :])

    for n in range(nb - 1):
        @pl.when(n < i)
        def _(n=n):
            s = _dot_nt(q4, k_ref[n * blk:(n + 1) * blk, :])
            s = jnp.where(sel_ref[:, n:n + 1] > 0.5, s, MASKED_LOGIT)
            m_old = m_ref[...]
            m_new = jnp.maximum(m_old, jnp.max(s, axis=1, keepdims=True))
            alpha = jnp.exp(m_old - m_new)
            p = jnp.exp(s - m_new)
            l_ref[...] = alpha * l_ref[...] + jnp.sum(p, axis=1, keepdims=True)
            acc_ref[...] = alpha * acc_ref[...] + _dot(p.astype(_BF16), v_ref[n * blk:(n + 1) * blk, :])
            m_ref[...] = m_new

    o4 = acc_ref[...] / l_ref[...]
    out = jnp.zeros((blk, width), _F32)
    for h in range(nh):
        out = out + jnp.where(lane_head == h, o4[h * blk:(h + 1) * blk, :], 0.0)
    o_ref[...] = out.astype(o_ref.dtype)


def _moba_prompt_call(qb, kb, vb, kmean, *, head_dim):
    b, l, d_attn = qb.shape
    blk = MOBA_BLOCK
    nb = l // blk
    width = min(V7X_MXU_DIM, d_attn)
    rows = (width // head_dim) * blk
    body = functools.partial(_moba_prompt_kernel, nb=nb, head_dim=head_dim, blk=blk)
    return pl.pallas_call(
        body,
        grid=(b, d_attn // width, nb),
        in_specs=[pl.BlockSpec((None, blk, width), lambda bi, c, i: (bi, i, c)),
                  pl.BlockSpec((None, l, width), lambda bi, c, i: (bi, 0, c)),
                  pl.BlockSpec((None, l, width), lambda bi, c, i: (bi, 0, c)),
                  pl.BlockSpec((None, nb, width), lambda bi, c, i: (bi, 0, c))],
        out_specs=pl.BlockSpec((None, blk, width), lambda bi, c, i: (bi, i, c)),
        out_shape=jax.ShapeDtypeStruct((b, l, d_attn), _BF16),
        scratch_shapes=[pltpu.VMEM((rows, V7X_LANES), _F32),
                        pltpu.VMEM((rows, 1), _F32),
                        pltpu.VMEM((rows, 1), _F32),
                        pltpu.VMEM((rows, width), _F32)],
        compiler_params=_cparams("parallel", "parallel", "arbitrary"),
        name="moba_prompt",
    )(qb, kb, vb, kmean)


def _kmean_pages_kernel(pt_ref, *refs, pages_per_block, inv_blk):
    del pt_ref
    page_refs, out_ref = refs[:-1], refs[-1]
    sums = [jnp.sum(r[...], axis=0) for r in page_refs]
    for j in range(len(page_refs) // pages_per_block):
        tot = sums[j * pages_per_block]
        for r in range(1, pages_per_block):
            tot = tot + sums[j * pages_per_block + r]
        out_ref[j] = tot * inv_blk


def _kmean_pages_call(cache_k, page_table, layer, *, pages_per_step):
    _, _, page, heads, head_dim = cache_k.shape
    bd, n_pages = page_table.shape
    ppb = MOBA_BLOCK // page
    n_blocks = n_pages // ppb
    steps = n_pages // pages_per_step
    bps = pages_per_step // ppb

    def page_spec(r):
        return pl.BlockSpec((None, None, page, heads, head_dim),
                            lambda b, c, pt: (layer, pt[b, c * pages_per_step + r], 0, 0, 0))

    grid_spec = pltpu.PrefetchScalarGridSpec(
        num_scalar_prefetch=1,
        grid=(bd, steps),
        in_specs=[page_spec(r) for r in range(pages_per_step)],
        out_specs=pl.BlockSpec((None, bps, heads, head_dim), lambda b, c, pt: (b, c, 0, 0)),
    )
    body = functools.partial(_kmean_pages_kernel, pages_per_block=ppb, inv_blk=1.0 / MOBA_BLOCK)
    return pl.pallas_call(
        body,
        grid_spec=grid_spec,
        out_shape=jax.ShapeDtypeStruct((bd, n_blocks, heads, head_dim), _F32),
        compiler_params=_cparams("parallel", "arbitrary"),
        name="kmean_pages",
    )(page_table, *([cache_k] * pages_per_step))


def _select_kernel(qh_ref, km_ref, sel_ref):
    rows = qh_ref.shape[0]
    nbp = km_ref.shape[0]
    gate = _dot_nt(qh_ref[...], km_ref[...].astype(_BF16))
    n_iota = lax.broadcasted_iota(jnp.int32, (rows, nbp), 1)
    rank = jnp.zeros((rows, nbp), jnp.int32)
    for m in range(nbp):
        gm = gate[:, m:m + 1]
        tie_first = jnp.where(m < n_iota, 1, 0)
        rank = rank + jnp.where(gm > gate, 1, jnp.where(gm == gate, tie_first, 0))
    lane = lax.broadcasted_iota(jnp.int32, (rows, V7X_LANES), 1)
    out = jnp.zeros((rows, V7X_LANES), jnp.int32)
    for j in range(MOBA_TOPK):
        idx = jnp.sum(jnp.where(rank == j, n_iota, 0), axis=1, keepdims=True)
        out = jnp.where(lane == j, idx, out)
    sel_ref[...] = out


def _select_call(q_s, kmean2d, *, heads, head_dim):
    bd, s_len, d_attn = q_s.shape
    nbp = kmean2d.shape[1]
    rows = heads * s_len
    head_of_lane = jnp.arange(d_attn) // head_dim
    qh = jnp.where(head_of_lane[None, None, None, :] == jnp.arange(heads)[None, :, None, None],
                   q_s[:, None, :, :], jnp.zeros((), q_s.dtype)).reshape(bd, rows, d_attn)
    return pl.pallas_call(
        _select_kernel,
        grid=(bd,),
        in_specs=[pl.BlockSpec((None, rows, d_attn), lambda b: (b, 0, 0)),
                  pl.BlockSpec((None, nbp, d_attn), lambda b: (b, 0, 0))],
        out_specs=pl.BlockSpec((None, rows, V7X_LANES), lambda b: (b, 0, 0)),
        out_shape=jax.ShapeDtypeStruct((bd, rows, V7X_LANES), jnp.int32),
        compiler_params=_cparams("parallel"),
        name="moba_select",
    )(qh, kmean2d)


def _sample_attend_kernel(pt_ref, sel_ref, q_ref, kn_ref, vn_ref, ck_ref, cv_ref, o_ref,
                          kbuf, vbuf, sem, *, layer, heads, s_len, page, ppb):
    b = pl.program_id(0)
    h = pl.program_id(1)
    n_sel = s_len * MOBA_TOPK
    blk = page * ppb

    def copies(slot, r):
        blk_idx = sel_ref[((b * heads + h) * s_len + slot // MOBA_TOPK) * MOBA_TOPK + slot % MOBA_TOPK]
        pg = pt_ref[b, blk_idx * ppb + r]
        dst = pl.ds(slot * blk + r * page, page)
        return (pltpu.make_async_copy(ck_ref.at[layer, pg, :, h, :], kbuf.at[dst, :], sem.at[0]),
                pltpu.make_async_copy(cv_ref.at[layer, pg, :, h, :], vbuf.at[dst, :], sem.at[1]))

    for slot in range(n_sel):
        for r in range(ppb):
            ck, cv = copies(slot, r)
            ck.start()
            cv.start()
    for slot in range(n_sel):
        for r in range(ppb):
            ck, cv = copies(slot, r)
            ck.wait()
            cv.wait()

    q = q_ref[...]
    s_past = _dot_nt(q, kbuf[...].astype(_BF16))
    qi = lax.broadcasted_iota(jnp.int32, s_past.shape, 0)
    slot_q = lax.broadcasted_iota(jnp.int32, s_past.shape, 1) // (blk * MOBA_TOPK)
    s_past = jnp.where(qi == slot_q, s_past, MASKED_LOGIT)
    s_own = _dot_nt(q, kn_ref[...])
    causal = (lax.broadcasted_iota(jnp.int32, s_own.shape, 1)
              <= lax.broadcasted_iota(jnp.int32, s_own.shape, 0))
    s_own = jnp.where(causal, s_own, MASKED_LOGIT)
    m = jnp.maximum(jnp.max(s_past, axis=1, keepdims=True), jnp.max(s_own, axis=1, keepdims=True))
    p_past = jnp.exp(s_past - m)
    p_own = jnp.exp(s_own - m)
    denom = jnp.sum(p_past, axis=1, keepdims=True) + jnp.sum(p_own, axis=1, keepdims=True)
    o = _dot(p_past.astype(_BF16), vbuf[...].astype(_BF16)) + _dot(p_own.astype(_BF16), vn_ref[...])
    o_ref[...] = (o / denom).astype(o_ref.dtype)


def _sample_attend_call(page_table, sel, q_h, kn_h, vn_h, cache_k, cache_v, layer, *, page):
    bd, heads, s_len, head_dim = q_h.shape
    ppb = MOBA_BLOCK // page
    n_rows = s_len * MOBA_TOPK * MOBA_BLOCK
    body = functools.partial(_sample_attend_kernel, layer=layer, heads=heads, s_len=s_len, page=page, ppb=ppb)
    per_head = pl.BlockSpec((None, None, s_len, head_dim), lambda b, h, pt, sl: (b, h, 0, 0))
    grid_spec = pltpu.PrefetchScalarGridSpec(
        num_scalar_prefetch=2,
        grid=(bd, heads),
        in_specs=[per_head, per_head, per_head,
                  pl.BlockSpec(memory_space=pl.ANY), pl.BlockSpec(memory_space=pl.ANY)],
        out_specs=per_head,
        scratch_shapes=[pltpu.VMEM((n_rows, head_dim), _F32),
                        pltpu.VMEM((n_rows, head_dim), _F32),
                        pltpu.SemaphoreType.DMA((2,))],
    )
    return pl.pallas_call(
        body,
        grid_spec=grid_spec,
        out_shape=jax.ShapeDtypeStruct((bd, heads, s_len, head_dim), _BF16),
        compiler_params=_cparams("arbitrary", "arbitrary"),
        name="moba_sample",
    )(page_table, sel, q_h, kn_h, vn_h, cache_k, cache_v)


def _ssm_prep_kernel(lr_row, li_row, lr_col, li_col, ldt, br_ref, bi_ref, cr_ref, ci_ref,
                     ca_re, ca_im, bp_re, bp_im, kt_ref, ap_re, ap_im, *, n_lag):
    dt = jnp.exp(ldt[...])

    def discretise(lr, li):
        mag = jnp.exp(lr * dt)
        a_re, a_im = mag * jnp.cos(li * dt), mag * jnp.sin(li * dt)
        den = lr * lr + li * li
        f_re = ((a_re - 1.0) * lr + a_im * li) / den
        f_im = (a_im * lr - (a_re - 1.0) * li) / den
        return a_re, a_im, f_re, f_im

    ar_r, ai_r, _, _ = discretise(lr_row[...], li_row[...])
    ar_c, ai_c, f_re, f_im = discretise(lr_col[...], li_col[...])
    br, bi = br_ref[...], bi_ref[...]
    bb_re = f_re * br - f_im * bi
    bb_im = f_re * bi + f_im * br
    cr, ci = cr_ref[...], ci_ref[...]

    pr_r, pi_r = jnp.ones_like(ar_r), jnp.zeros_like(ar_r)
    pr_c, pi_c = jnp.ones_like(ar_c), jnp.zeros_like(ar_c)
    for lag in range(n_lag + 1):
        car = cr * pr_r - ci * pi_r
        cai = cr * pi_r + ci * pr_r
        ca_re[lag] = car
        ca_im[lag] = cai
        ap_re[lag] = pr_r
        ap_im[lag] = pi_r
        if lag < n_lag:
            bp_re[lag] = pr_c * bb_re - pi_c * bb_im
            bp_im[lag] = pr_c * bb_im + pi_c * bb_re
            kt_ref[lag] = (jnp.dot(car, bb_re, preferred_element_type=_F32, precision=lax.Precision.HIGHEST)
                           - jnp.dot(cai, bb_im, preferred_element_type=_F32, precision=lax.Precision.HIGHEST))
        pr_r, pi_r = pr_r * ar_r - pi_r * ai_r, pr_r * ai_r + pi_r * ar_r
        pr_c, pi_c = pr_c * ar_c - pi_c * ai_c, pr_c * ai_c + pi_c * ar_c


def _ssm_prep_call(lam_re, lam_im, log_dt, b_re, b_im, c_re, c_im):
    g, p = lam_re.shape
    hg = b_re.shape[-1]
    n_lag = SSM_CHUNK
    body = functools.partial(_ssm_prep_kernel, n_lag=n_lag)
    row = pl.BlockSpec((None, 1, p), lambda i: (i, 0, 0))
    col = pl.BlockSpec((None, p, 1), lambda i: (i, 0, 0))

    def full(*dims):
        return pl.BlockSpec((None,) + dims, lambda i: (i,) + (0,) * len(dims))

    return pl.pallas_call(
        body,
        grid=(g,),
        in_specs=[row, row, col, col, full(1, 1), full(p, hg), full(p, hg), full(hg, p), full(hg, p)],
        out_specs=[full(n_lag + 1, hg, p), full(n_lag + 1, hg, p), full(n_lag, p, hg), full(n_lag, p, hg),
                   full(n_lag, hg, hg), full(n_lag + 1, 1, p), full(n_lag + 1, 1, p)],
        out_shape=[jax.ShapeDtypeStruct((g, n_lag + 1, hg, p), _F32),
                   jax.ShapeDtypeStruct((g, n_lag + 1, hg, p), _F32),
                   jax.ShapeDtypeStruct((g, n_lag, p, hg), _F32),
                   jax.ShapeDtypeStruct((g, n_lag, p, hg), _F32),
                   jax.ShapeDtypeStruct((g, n_lag, hg, hg), _F32),
                   jax.ShapeDtypeStruct((g, n_lag + 1, 1, p), _F32),
                   jax.ShapeDtypeStruct((g, n_lag + 1, 1, p), _F32)],
        compiler_params=_cparams("parallel"),
        name="ssm_prep",
    )(lam_re[:, None, :], lam_im[:, None, :], lam_re[:, :, None], lam_im[:, :, None],
      log_dt[:, None, None], b_re, b_im, c_re, c_im)


def _ssm_tables(prep, d_skip, n_tok):
    ca_re, ca_im, bp_re, bp_im, kt, ap_re, ap_im = prep
    g, _, hg, p = ca_re.shape
    tc = SSM_CHUNK
    w = tc * hg
    s_idx = jnp.arange(tc)[:, None]
    t_idx = jnp.arange(tc)[None, :]
    lag = t_idx - s_idx
    toep = jnp.where((lag >= 0)[None, :, :, None, None], kt[:, jnp.clip(lag, 0, tc - 1)], 0.0)
    m_intra = toep.transpose(0, 1, 4, 2, 3).reshape(g, w, w)
    end_lag = n_tok - 1 - jnp.arange(tc)
    live = (end_lag >= 0)[None, :, None, None]
    end_lag = jnp.clip(end_lag, 0, tc - 1)
    bend_re = jnp.where(live, bp_re[:, end_lag], 0.0).transpose(0, 1, 3, 2).reshape(g, w, p)
    bend_im = jnp.where(live, bp_im[:, end_lag], 0.0).transpose(0, 1, 3, 2).reshape(g, w, p)
    cp_re = ca_re[:, 1:tc + 1].transpose(0, 3, 1, 2).reshape(g, p, w)
    cp_im = (-ca_im[:, 1:tc + 1]).transpose(0, 3, 1, 2).reshape(g, p, w)
    dvec = jnp.tile(d_skip[:, None, :], (1, tc, 1)).reshape(g, 1, w)
    return (m_intra.astype(_BF16), bend_re.astype(_BF16), bend_im.astype(_BF16),
            cp_re.astype(_BF16), cp_im.astype(_BF16), ap_re[:, n_tok], ap_im[:, n_tok], dvec)


def _ssm_kernel(x_ref, mt_ref, ber_ref, bei_ref, cpr_ref, cpi_ref, aer_ref, aei_ref, dv_ref,
                h0r_ref, h0i_ref, y_ref, hr_ref, hi_ref, sr, si, hsr, hsi, *, gb, nc, bp):
    for g in range(gb):
        xb = x_ref[g].astype(_BF16)
        sr[g] = _dot(xb, ber_ref[g])
        si[g] = _dot(xb, bei_ref[g])

    a_re, a_im = aer_ref[...], aei_ref[...]

    def step(j, carry):
        h_re, h_im = carry
        rows = pl.ds(pl.multiple_of(j * bp, bp), bp)
        hsr[:, rows, :] = h_re
        hsi[:, rows, :] = h_im
        return (a_re * h_re - a_im * h_im + sr[:, rows, :],
                a_re * h_im + a_im * h_re + si[:, rows, :])

    h_re, h_im = lax.fori_loop(0, nc, step, (h0r_ref[...], h0i_ref[...]))
    hr_ref[...] = h_re
    hi_ref[...] = h_im

    for g in range(gb):
        x = x_ref[g]
        y = _dot(x.astype(_BF16), mt_ref[g])
        y = y + _dot(hsr[g].astype(_BF16), cpr_ref[g]) + _dot(hsi[g].astype(_BF16), cpi_ref[g])
        y_ref[g] = (y + x * dv_ref[g]).astype(y_ref.dtype)


def _ssm_call(xg, tables, h0_re, h0_im, *, nc, bp, gb):
    m_intra, bend_re, bend_im, cp_re, cp_im, aend_re, aend_im, dvec = tables
    g, n, w = xg.shape
    p = bend_re.shape[-1]
    body = functools.partial(_ssm_kernel, gb=gb, nc=nc, bp=bp)

    def spec(*dims):
        return pl.BlockSpec((gb,) + dims, lambda i: (i,) + (0,) * len(dims))

    return pl.pallas_call(
        body,
        grid=(g // gb,),
        in_specs=[spec(n, w), spec(w, w), spec(w, p), spec(w, p), spec(p, w), spec(p, w),
                  spec(1, p), spec(1, p), spec(1, w), spec(bp, p), spec(bp, p)],
        out_specs=[spec(n, w), spec(bp, p), spec(bp, p)],
        out_shape=[jax.ShapeDtypeStruct((g, n, w), _BF16),
                   jax.ShapeDtypeStruct((g, bp, p), _F32),
                   jax.ShapeDtypeStruct((g, bp, p), _F32)],
        scratch_shapes=[pltpu.VMEM((gb, n, p), _F32)] * 4,
        compiler_params=_cparams("parallel"),
        name="ssm",
    )(xg, m_intra, bend_re, bend_im, cp_re, cp_im, aend_re, aend_im, dvec, h0_re, h0_im)


def _ssm_branch(u, tables, h0_re, h0_im, *, gb):
    b, l, d_ssm = u.shape
    g, _, p = tables[1].shape
    hg = d_ssm // g
    tc = SSM_CHUNK
    nc = -(-l // tc)
    bp = -(-b // 8) * 8
    up = jnp.pad(u, ((0, bp - b), (0, nc * tc - l), (0, 0)))
    xg = up.reshape(bp, nc, tc, g, hg).transpose(3, 1, 0, 2, 4).reshape(g, nc * bp, tc * hg)
    pad_state = lambda h: jnp.pad(h.astype(_F32), ((0, bp - b), (0, 0), (0, 0))).transpose(1, 0, 2)
    y, h_re, h_im = _ssm_call(xg, tables, pad_state(h0_re), pad_state(h0_im), nc=nc, bp=bp, gb=gb)
    y = y.reshape(g, nc, bp, tc, hg).transpose(2, 1, 3, 0, 4).reshape(bp, nc * tc, d_ssm)[:b, :l]
    return y, h_re.transpose(1, 0, 2)[:b], h_im.transpose(1, 0, 2)[:b]


def _merge_kernel(x_ref, oa_ref, ys_ref, g_ref, wg_ref, wglu_ref, wpa_ref, wpb_ref, wout_ref, fg_ref,
                  o_ref, *, d_attn, d_ssm, final_norm):
    x = x_ref[...]
    d = x.shape[-1]
    hb = _rmsnorm(x, g_ref[...]).astype(_BF16)
    z_a = _dot(hb, wg_ref[:, 0:d_attn])
    ya = oa_ref[...].astype(_F32) * jax.nn.silu(z_a)
    br_a = _dot(ya.astype(_BF16), wpa_ref[...])
    glu = _dot(jax.nn.gelu(ys_ref[...].astype(_F32)).astype(_BF16), wglu_ref[...])
    z_b = _dot(hb, wg_ref[:, d_attn:d_attn + d_ssm])
    yb = glu[:, :d_ssm] * jax.nn.sigmoid(glu[:, d_ssm:]) * jax.nn.silu(z_b)
    br_b = _dot(yb.astype(_BF16), wpb_ref[...])
    g_a = _dot(hb, wg_ref[:, d_attn + d_ssm:d_attn + d_ssm + d])
    g_b = _dot(hb, wg_ref[:, d_attn + d_ssm + d:d_attn + d_ssm + 2 * d])
    merged = jax.nn.sigmoid(g_a) * br_a + jax.nn.sigmoid(g_b) * br_b
    out = x + _dot(merged.astype(_BF16), wout_ref[...])
    if final_norm:
        out = _rmsnorm(out, fg_ref[...])
    o_ref[...] = out


def _merge_call(x2d, oa, ys, g, w_g, w_glu, w_pa, w_pb, w_out, final_g, *, tm, final_norm):
    t, d = x2d.shape
    d_attn, d_ssm = oa.shape[1], ys.shape[1]
    row = lambda i: (i, 0)
    const = lambda a: pl.BlockSpec(a.shape, lambda i: (0, 0), pipeline_mode=pl.Buffered(1))
    body = functools.partial(_merge_kernel, d_attn=d_attn, d_ssm=d_ssm, final_norm=final_norm)
    return pl.pallas_call(
        body,
        grid=(t // tm,),
        in_specs=[pl.BlockSpec((tm, d), row), pl.BlockSpec((tm, d_attn), row), pl.BlockSpec((tm, d_ssm), row),
                  const(g), const(w_g), const(w_glu), const(w_pa), const(w_pb), const(w_out), const(final_g)],
        out_specs=pl.BlockSpec((tm, d), row),
        out_shape=jax.ShapeDtypeStruct((t, d), _F32),
        compiler_params=_cparams("parallel"),
        name="merge",
    )(x2d, oa, ys, g, w_g, w_glu, w_pa, w_pb, w_out, final_g)


def _largest_tile(t, cap):
    tm = min(t, cap)
    while t % tm:
        tm //= 2
    return tm


def kernel(x_prompt, x_sample, cache_k, cache_v, state_ssm_re, state_ssm_im, page_table, norm_g, w_in,
           lam_re, lam_im, log_dt, b_re, b_im, c_re, c_im, d_skip, w_glu, w_pa, w_pb, w_out, final_norm_g):
    b, l, d = x_prompt.shape
    bd, s_len, _ = x_sample.shape
    depth = norm_g.shape[0]
    _, n_pool, page, heads, head_dim = cache_k.shape
    n_groups, p_state = lam_re.shape[1:]
    hg = b_re.shape[-1]
    d_attn, d_ssm = heads * head_dim, n_groups * hg
    n_pages = page_table.shape[1]
    assert w_in.shape[-1] == 4 * d_attn + 2 * d_ssm + 2 * d
    assert l % MOBA_BLOCK == 0 and MOBA_BLOCK % page == 0 and (n_pages * page) % MOBA_BLOCK == 0
    assert s_len <= SSM_CHUNK and SSM_CHUNK * hg == V7X_MXU_DIM and l % SSM_CHUNK == 0

    tm_p = _largest_tile(b * l, 512)
    assert tm_p % MOBA_BLOCK == 0
    tm_s = bd * s_len
    xp = x_prompt.reshape(b * l, d)
    xs = x_sample.reshape(bd * s_len, d)
    zeros_state = jnp.zeros((b, n_groups, p_state), _F32)
    fg = final_norm_g[None, :]

    kp, vp, ks, vs, hpr, hpi, hsr, hsi = ([] for _ in range(8))
    for layer in range(depth):
        wl = w_in[layer]
        o_z, o_u, o_zb, o_ga = 3 * d_attn, 4 * d_attn, 4 * d_attn + d_ssm, 4 * d_attn + 2 * d_ssm
        w_a = jnp.concatenate([wl[:, :o_z], wl[:, o_u:o_zb]], axis=1).astype(_BF16)
        w_g = jnp.concatenate([wl[:, o_z:o_u], wl[:, o_zb:o_ga], wl[:, o_ga:]], axis=1).astype(_BF16)
        gl = norm_g[layer][None, :]
        merge_w = (gl, w_g, w_glu[layer].astype(_BF16), w_pa[layer].astype(_BF16),
                   w_pb[layer].astype(_BF16), w_out[layer].astype(_BF16), fg)
        last = layer == depth - 1
        prep = _ssm_prep_call(lam_re[layer], lam_im[layer], log_dt[layer], b_re[layer], b_im[layer],
                              c_re[layer], c_im[layer])

        qb, k, v, kb, vb, u, kmean = _proj_call(xp, gl, w_a, d_attn=d_attn, d_ssm=d_ssm, head_dim=head_dim,
                                                tm=tm_p, with_kmean=True)
        oa = _moba_prompt_call(qb.reshape(b, l, d_attn), kb.reshape(b, l, d_attn), vb.reshape(b, l, d_attn),
                               kmean.reshape(b, l // MOBA_BLOCK, d_attn), head_dim=head_dim)
        y, h_re, h_im = _ssm_branch(u.reshape(b, l, d_ssm), _ssm_tables(prep, d_skip[layer], SSM_CHUNK),
                                    zeros_state, zeros_state, gb=2)
        xp = _merge_call(xp, oa.reshape(b * l, d_attn), y.reshape(b * l, d_ssm), *merge_w,
                         tm=_largest_tile(b * l, 256), final_norm=last)
        kp.append(k.reshape(b, l, heads, head_dim))
        vp.append(v.reshape(b, l, heads, head_dim))
        hpr.append(h_re)
        hpi.append(h_im)

        qb, k, v, kb, vb, u = _proj_call(xs, gl, w_a, d_attn=d_attn, d_ssm=d_ssm, head_dim=head_dim,
                                         tm=tm_s, with_kmean=False)
        kmean_past = _kmean_pages_call(cache_k, page_table, layer, pages_per_step=min(16, n_pages))
        sel = _select_call(qb.reshape(bd, s_len, d_attn), kmean_past.reshape(bd, -1, d_attn),
                           heads=heads, head_dim=head_dim)
        sel = sel[:, :, :MOBA_TOPK].reshape(-1)
        to_heads = lambda a: a.reshape(bd, s_len, heads, head_dim).transpose(0, 2, 1, 3)
        oa = _sample_attend_call(page_table, sel, to_heads(qb), to_heads(kb), to_heads(vb),
                                 cache_k, cache_v, layer, page=page)
        oa = oa.transpose(0, 2, 1, 3).reshape(bd * s_len, d_attn)
        y, h_re, h_im = _ssm_branch(u.reshape(bd, s_len, d_ssm), _ssm_tables(prep, d_skip[layer], s_len),
                                    state_ssm_re[layer], state_ssm_im[layer], gb=2)
        xs = _merge_call(xs, oa, y.reshape(bd * s_len, d_ssm), *merge_w, tm=tm_s, final_norm=last)
        ks.append(k.reshape(bd, s_len, heads, head_dim))
        vs.append(v.reshape(bd, s_len, heads, head_dim))
        hsr.append(h_re)
        hsi.append(h_im)

    return (xp.reshape(b, l, d), xs.reshape(bd, s_len, d),
            jnp.stack(kp), jnp.stack(vp), jnp.stack(ks), jnp.stack(vs),
            jnp.stack(hpr), jnp.stack(hpi), jnp.stack(hsr), jnp.stack(hsi))
```

```python
import functools

import jax
import jax.numpy as jnp
from jax import lax
from jax.experimental import pallas as pl
from jax.experimental.pallas import tpu as pltpu

MOBA_BLOCK = 256
MOBA_TOPK = 3
RMS_EPS = 1e-6

V7X_LANES = 128
V7X_SUBLANES = 8
V7X_MXU_DIM = 256
V7X_VMEM_LIMIT_BYTES = 56 * 1024 * 1024

SSM_CHUNK = 16
MASKED_LOGIT = -1e30

_F32 = jnp.float32
_BF16 = jnp.bfloat16


def _cparams(*sem):
    return pltpu.CompilerParams(dimension_semantics=sem, vmem_limit_bytes=V7X_VMEM_LIMIT_BYTES)


def _rmsnorm(x, g):
    return x * lax.rsqrt(jnp.mean(x * x, axis=-1, keepdims=True) + RMS_EPS) * g


def _dot(a, b):
    return jnp.dot(a, b, preferred_element_type=_F32)


def _dot_nt(a, b):
    return lax.dot_general(a, b, (((1,), (1,)), ((), ())), preferred_element_type=_F32)


def _twice(a):
    return jnp.concatenate([a, a], axis=1)


def _tokens_to_chunks(u, slab_ref, xg_ref, *, hg, tok):
    tm, c = u.shape
    n = tm // tok
    gpl = V7X_LANES // hg
    lane_slot = lax.broadcasted_iota(jnp.int32, (n, V7X_LANES), 1) // hg
    for cb in range(c // V7X_LANES):
        slab_ref[cb] = u[:, cb * V7X_LANES:(cb + 1) * V7X_LANES]
    for cb in range(c // V7X_LANES):
        rows = [slab_ref[cb, pl.ds(t, n, stride=tok), :] for t in range(tok)]
        rolled = {}
        for gi in range(gpl):
            cols = []
            for cv in range(SSM_CHUNK // gpl):
                acc = jnp.zeros((n, V7X_LANES), _F32)
                for jj in range(gpl):
                    t = cv * gpl + jj
                    if t >= tok:
                        continue
                    shift = ((jj - gi) * hg) % V7X_LANES
                    if (t, shift) not in rolled:
                        rolled[(t, shift)] = pltpu.roll(rows[t], shift, 1) if shift else rows[t]
                    acc = jnp.where(lane_slot == jj, rolled[(t, shift)], acc)
                cols.append(acc)
            xg_ref[cb * gpl + gi] = jnp.concatenate(cols, axis=1)


def _chunks_to_tokens(y_ref, slab_ref, *, hg, tok):
    g, n, _ = y_ref.shape
    gpl = V7X_LANES // hg
    lane_slot = lax.broadcasted_iota(jnp.int32, (n, V7X_LANES), 1) // hg
    n_slabs = g // gpl
    for cb in range(n_slabs):
        for cv in range(SSM_CHUNK // gpl):
            src = [y_ref[cb * gpl + gi, :, cv * V7X_LANES:(cv + 1) * V7X_LANES] for gi in range(gpl)]
            rolled = {}
            for jj in range(gpl):
                t = cv * gpl + jj
                if t >= tok:
                    continue
                piece = jnp.zeros((n, V7X_LANES), _F32)
                for gi in range(gpl):
                    shift = ((gi - jj) * hg) % V7X_LANES
                    if (gi, shift) not in rolled:
                        rolled[(gi, shift)] = pltpu.roll(src[gi], shift, 1) if shift else src[gi]
                    piece = jnp.where(lane_slot == gi, rolled[(gi, shift)], piece)
                slab_ref[cb, pl.ds(t, n, stride=tok), :] = piece
    return jnp.concatenate([slab_ref[cb] for cb in range(n_slabs)], axis=1)


def _proj_kernel(x_ref, g_ref, w_ref, q_ref, k_ref, v_ref, kb_ref, vb_ref, xg_ref, km_ref, slab_ref,
                 *, d_attn, d_ssm, head_dim, blk, hg, tok):
    hb = _rmsnorm(x_ref[...], g_ref[...]).astype(_BF16)
    q = _dot(hb, w_ref[:, 0:d_attn])
    k = _dot(hb, w_ref[:, d_attn:2 * d_attn])
    v = _dot(hb, w_ref[:, 2 * d_attn:3 * d_attn])
    u = _dot(hb, w_ref[:, 3 * d_attn:3 * d_attn + d_ssm])
    q_ref[...] = (q * (head_dim ** -0.5)).astype(_BF16)
    k_ref[...] = k
    v_ref[...] = v
    kb_ref[...] = k.astype(_BF16)
    vb_ref[...] = v.astype(_BF16)
    _tokens_to_chunks(u, slab_ref, xg_ref, hg=hg, tok=tok)
    if km_ref is not None:
        tm = k.shape[0]
        km_ref[0] = jnp.sum(k.reshape(tm // blk, blk, d_attn), axis=1) * (1.0 / blk)


def _proj_call(x2d, g, w_a, *, d_attn, d_ssm, head_dim, hg, tok, tm, with_kmean):
    t, d = x2d.shape
    n_tiles = t // tm
    n_groups = d_ssm // hg
    row = lambda i: (i, 0)
    out_shape = [
        jax.ShapeDtypeStruct((t, d_attn), _BF16),
        jax.ShapeDtypeStruct((t, d_attn), _F32),
        jax.ShapeDtypeStruct((t, d_attn), _F32),
        jax.ShapeDtypeStruct((t, d_attn), _BF16),
        jax.ShapeDtypeStruct((t, d_attn), _BF16),
        jax.ShapeDtypeStruct((n_groups, t // tok, SSM_CHUNK * hg), _F32),
    ]
    out_specs = [pl.BlockSpec((tm, d_attn), row)] * 5 + [
        pl.BlockSpec((n_groups, tm // tok, SSM_CHUNK * hg), lambda i: (0, i, 0))]
    body = functools.partial(_proj_kernel, d_attn=d_attn, d_ssm=d_ssm, head_dim=head_dim, blk=MOBA_BLOCK,
                             hg=hg, tok=tok)
    if with_kmean:
        out_shape.append(jax.ShapeDtypeStruct((n_tiles, tm // MOBA_BLOCK, d_attn), _F32))
        out_specs.append(pl.BlockSpec((1, tm // MOBA_BLOCK, d_attn), lambda i: (i, 0, 0)))
        kern = body
    else:
        kern = lambda *refs: body(*refs[:-1], None, refs[-1])
    return pl.pallas_call(
        kern,
        grid=(n_tiles,),
        in_specs=[pl.BlockSpec((tm, d), row),
                  pl.BlockSpec((1, d), lambda i: (0, 0)),
                  pl.BlockSpec(w_a.shape, lambda i: (0, 0))],
        out_specs=out_specs,
        out_shape=out_shape,
        scratch_shapes=[pltpu.VMEM((d_ssm // V7X_LANES, tm, V7X_LANES), _F32)],
        compiler_params=_cparams("parallel"),
        name="proj",
    )(x2d, g, w_a)


def _moba_prompt_kernel(q_ref, k_ref, v_ref, km_ref, o_ref, kaug_ref, m_ref, l_ref, acc_ref,
                        *, nb, head_dim, blk):
    i = pl.program_id(2)
    width = q_ref.shape[-1]
    pw = V7X_LANES
    n_pairs = width // pw
    hpp = pw // head_dim
    prow = hpp * blk
    rows = n_pairs * prow
    seq = k_ref.shape[0]

    @pl.when(i == 0)
    def _():
        key_blk = lax.broadcasted_iota(jnp.int32, (seq, pw), 0) // blk
        onehot = jnp.where(key_blk == lax.broadcasted_iota(jnp.int32, (seq, pw), 1), 1.0, 0.0).astype(_BF16)
        for p in range(n_pairs):
            kaug_ref[p, :, 0:pw] = k_ref[:, p * pw:(p + 1) * pw]
            kaug_ref[p, :, pw:2 * pw] = onehot

    q = q_ref[...]
    lane_head = lax.broadcasted_iota(jnp.int32, (blk, pw), 1) // head_dim
    q2 = []
    for p in range(n_pairs):
        qp = q[:, p * pw:(p + 1) * pw]
        q2.append(jnp.concatenate([jnp.where(lane_head == h, qp, jnp.zeros_like(qp)) for h in range(hpp)], axis=0))

    km = km_ref[...].astype(_BF16)
    gate = jnp.concatenate([_dot_nt(km[:, p * pw:(p + 1) * pw], q2[p]) for p in range(n_pairs)], axis=1)
    n_iota = lax.broadcasted_iota(jnp.int32, (nb, rows), 0)
    gate = jnp.where(n_iota < i, gate, -jnp.inf)
    rank = jnp.zeros((nb, rows), jnp.int32)
    for m in range(nb):
        gm = gate[m:m + 1, :]
        tie_first = jnp.where(m < n_iota, 1, 0)
        rank = rank + jnp.where(gm > gate, 1, jnp.where(gm == gate, tie_first, 0))
    keep = jnp.where(n_iota < i, jnp.where(rank < MOBA_TOPK, 1, 0), jnp.where(n_iota == i, 1, 0))
    bias = jnp.where(keep == 1, 0.0, MASKED_LOGIT).astype(_F32)
    bias = jnp.concatenate([bias, jnp.full((pw - nb, rows), MASKED_LOGIT, _F32)], axis=0).T
    qaug = [jnp.concatenate([q2[p], bias[p * prow:(p + 1) * prow].astype(_BF16)], axis=1) for p in range(n_pairs)]

    def logits(kstart):
        return jnp.concatenate([_dot_nt(qaug[p], kaug_ref[p, pl.ds(kstart, blk), :]) for p in range(n_pairs)],
                               axis=0)

    def row_max(s):
        half = jnp.maximum(s[:, :pw], s[:, pw:])
        return jnp.broadcast_to(jnp.max(half, axis=1, keepdims=True), (rows, pw))

    start = pl.multiple_of(i * blk, blk)
    s = logits(start)
    q_off = lax.broadcasted_iota(jnp.int32, (rows, blk), 0) % blk
    k_off = lax.broadcasted_iota(jnp.int32, (rows, blk), 1)
    s = jnp.where(k_off <= q_off, s, MASKED_LOGIT)
    m0 = row_max(s)
    p0 = jnp.exp(s - _twice(m0))
    m_ref[...] = m0
    l_ref[...] = p0[:, :pw] + p0[:, pw:]
    acc_ref[...] = _dot(p0.astype(_BF16), v_ref[pl.ds(start, blk), :])

    for n in range(nb - 1):
        @pl.when(n < i)
        def _(n=n):
            s = logits(n * blk)
            m_old = m_ref[...]
            m_new = jnp.maximum(m_old, row_max(s))
            alpha = jnp.exp(m_old - m_new)
            p = jnp.exp(s - _twice(m_new))
            l_ref[...] = alpha * l_ref[...] + (p[:, :pw] + p[:, pw:])
            acc_ref[...] = _twice(alpha) * acc_ref[...] + _dot(p.astype(_BF16), v_ref[n * blk:(n + 1) * blk, :])
            m_ref[...] = m_new

    o4 = acc_ref[...] / jnp.sum(l_ref[...], axis=1, keepdims=True)
    lane_head4 = lax.broadcasted_iota(jnp.int32, (blk, width), 1) // head_dim
    out = jnp.zeros((blk, width), _F32)
    for h in range(width // head_dim):
        out = out + jnp.where(lane_head4 == h, o4[h * blk:(h + 1) * blk, :], 0.0)
    o_ref[...] = out.astype(o_ref.dtype)


def _moba_prompt_call(qb, kb, vb, kmean, *, head_dim):
    b, l, d_attn = qb.shape
    blk = MOBA_BLOCK
    nb = l // blk
    width = V7X_MXU_DIM
    rows = (width // head_dim) * blk
    body = functools.partial(_moba_prompt_kernel, nb=nb, head_dim=head_dim, blk=blk)
    return pl.pallas_call(
        body,
        grid=(b, d_attn // width, nb),
        in_specs=[pl.BlockSpec((None, blk, width), lambda bi, c, i: (bi, i, c)),
                  pl.BlockSpec((None, l, width), lambda bi, c, i: (bi, 0, c)),
                  pl.BlockSpec((None, l, width), lambda bi, c, i: (bi, 0, c)),
                  pl.BlockSpec((None, nb, width), lambda bi, c, i: (bi, 0, c))],
        out_specs=pl.BlockSpec((None, blk, width), lambda bi, c, i: (bi, i, c)),
        out_shape=jax.ShapeDtypeStruct((b, l, d_attn), _BF16),
        scratch_shapes=[pltpu.VMEM((width // V7X_LANES, l, 2 * V7X_LANES), _BF16),
                        pltpu.VMEM((rows, V7X_LANES), _F32),
                        pltpu.VMEM((rows, V7X_LANES), _F32),
                        pltpu.VMEM((rows, width), _F32)],
        compiler_params=_cparams("parallel", "parallel", "arbitrary"),
        name="moba_prompt",
    )(qb, kb, vb, kmean)


def _kmean_pages_kernel(pt_ref, *refs, pages_per_block, inv_blk):
    del pt_ref
    page_refs, out_ref = refs[:-1], refs[-1]
    c = pl.program_id(1)
    bps = len(page_refs) // pages_per_block

    @pl.when(c == 0)
    def _():
        out_ref[...] = jnp.zeros_like(out_ref)

    lane = lax.broadcasted_iota(jnp.int32, out_ref.shape, 2)
    acc = out_ref[...]
    for j in range(bps):
        tot = page_refs[j * pages_per_block][...]
        for r in range(1, pages_per_block):
            tot = tot + page_refs[j * pages_per_block + r][...]
        mean = jnp.sum(tot, axis=-1, keepdims=True) * inv_blk
        acc = jnp.where(lane == c * bps + j, mean, acc)
    out_ref[...] = acc


def _kmean_pages_call(cache_kt, page_table, layer, *, pages_per_step):
    _, _, heads, head_dim, page = cache_kt.shape
    bd, n_pages = page_table.shape
    ppb = MOBA_BLOCK // page
    assert n_pages // ppb <= V7X_LANES and n_pages % pages_per_step == 0 and pages_per_step % ppb == 0
    steps = n_pages // pages_per_step

    def page_spec(r):
        return pl.BlockSpec((None, None, heads, head_dim, page),
                            lambda b, c, pt: (layer, pt[b, c * pages_per_step + r], 0, 0, 0))

    grid_spec = pltpu.PrefetchScalarGridSpec(
        num_scalar_prefetch=1,
        grid=(bd, steps),
        in_specs=[page_spec(r) for r in range(pages_per_step)],
        out_specs=pl.BlockSpec((None, heads, head_dim, V7X_LANES), lambda b, c, pt: (b, 0, 0, 0)),
    )
    body = functools.partial(_kmean_pages_kernel, pages_per_block=ppb, inv_blk=1.0 / MOBA_BLOCK)
    return pl.pallas_call(
        body,
        grid_spec=grid_spec,
        out_shape=jax.ShapeDtypeStruct((bd, heads, head_dim, V7X_LANES), _F32),
        compiler_params=_cparams("parallel", "arbitrary"),
        name="kmean_pages",
    )(page_table, *([cache_kt] * pages_per_step))


def _select_kernel(qh_ref, kmt_ref, sel_ref, *, nbp):
    rows = qh_ref.shape[0]
    gate = _dot(qh_ref[...], kmt_ref[...].astype(_BF16))
    n_iota = lax.broadcasted_iota(jnp.int32, (rows, V7X_LANES), 1)
    gate = jnp.where(n_iota < nbp, gate, -jnp.inf)
    rank = jnp.zeros((rows, V7X_LANES), jnp.int32)
    for m in range(nbp):
        gm = gate[:, m:m + 1]
        tie_first = jnp.where(m < n_iota, 1, 0)
        rank = rank + jnp.where(gm > gate, 1, jnp.where(gm == gate, tie_first, 0))
    out = jnp.zeros((rows, V7X_LANES), jnp.int32)
    for j in range(MOBA_TOPK):
        hit = jnp.where(n_iota < nbp, jnp.where(rank == j, n_iota, 0), 0)
        out = jnp.where(n_iota == j, jnp.sum(hit, axis=1, keepdims=True), out)
    sel_ref[...] = out


def _select_call(q_s, kmean_t, *, heads, head_dim, nbp):
    bd, s_len, d_attn = q_s.shape
    rows = heads * s_len
    head_of_lane = jnp.arange(d_attn) // head_dim
    qh = jnp.where(head_of_lane[None, None, None, :] == jnp.arange(heads)[None, :, None, None],
                   q_s[:, None, :, :], jnp.zeros((), q_s.dtype)).reshape(bd, rows, d_attn)
    return pl.pallas_call(
        functools.partial(_select_kernel, nbp=nbp),
        grid=(bd,),
        in_specs=[pl.BlockSpec((None, rows, d_attn), lambda b: (b, 0, 0)),
                  pl.BlockSpec((None, d_attn, V7X_LANES), lambda b: (b, 0, 0))],
        out_specs=pl.BlockSpec((None, rows, V7X_LANES), lambda b: (b, 0, 0)),
        out_shape=jax.ShapeDtypeStruct((bd, rows, V7X_LANES), jnp.int32),
        compiler_params=_cparams("parallel"),
        name="moba_select",
    )(qh, kmean_t)


def _sample_attend_kernel(pt_ref, sel_ref, q_ref, kn_ref, vn_ref, ck_ref, cv_ref, o_ref, kbuf, vbuf, sem,
                          *, layer, heads, s_len, page, ppb):
    i = pl.program_id(0)
    n_steps = pl.num_programs(0)
    n_sel = s_len * MOBA_TOPK
    blk = page * ppb

    def copies(step, buf):
        b = step // heads
        h = step % heads
        out = []
        for slot in range(n_sel):
            blk_idx = sel_ref[step * n_sel + slot]
            for r in range(ppb):
                pg = pt_ref[b, blk_idx * ppb + r]
                dst = pl.ds((slot * ppb + r) * page, page)
                out.append(pltpu.make_async_copy(ck_ref.at[layer, pg, h], kbuf.at[buf, :, dst], sem.at[0, buf]))
                out.append(pltpu.make_async_copy(cv_ref.at[layer, pg, h], vbuf.at[buf, :, dst], sem.at[1, buf]))
        return out

    def attend(buf):
        q = q_ref[...]
        s_past = _dot(q, kbuf[buf].astype(_BF16))
        qi = lax.broadcasted_iota(jnp.int32, s_past.shape, 0)
        slot_q = lax.broadcasted_iota(jnp.int32, s_past.shape, 1) // (blk * MOBA_TOPK)
        s_past = jnp.where(qi == slot_q, s_past, MASKED_LOGIT)
        s_own = _dot_nt(q, kn_ref[...])
        causal = (lax.broadcasted_iota(jnp.int32, s_own.shape, 1)
                  <= lax.broadcasted_iota(jnp.int32, s_own.shape, 0))
        s_own = jnp.where(causal, s_own, MASKED_LOGIT)
        m = jnp.maximum(jnp.max(s_past, axis=1, keepdims=True), jnp.max(s_own, axis=1, keepdims=True))
        p_past = jnp.exp(s_past - m)
        p_own = jnp.exp(s_own - m)
        denom = jnp.sum(p_past, axis=1, keepdims=True) + jnp.sum(p_own, axis=1, keepdims=True)
        o = _dot_nt(p_past.astype(_BF16), vbuf[buf].astype(_BF16)) + _dot(p_own.astype(_BF16), vn_ref[...])
        o_ref[...] = (o / denom).astype(o_ref.dtype)

    @pl.when(i == 0)
    def _():
        for cp in copies(0, 0):
            cp.start()

    for parity in range(2):
        @pl.when(i % 2 == parity)
        def _(parity=parity):
            @pl.when(i + 1 < n_steps)
            def _():
                for cp in copies(i + 1, 1 - parity):
                    cp.start()
            for cp in copies(i, parity):
                cp.wait()
            attend(parity)


def _sample_attend_call(page_table, sel, q_h, kn_h, vn_h, cache_kt, cache_vt, layer):
    bd, heads, s_len, head_dim = q_h.shape
    page = cache_kt.shape[-1]
    ppb = MOBA_BLOCK // page
    n_cols = s_len * MOBA_TOPK * MOBA_BLOCK
    body = functools.partial(_sample_attend_kernel, layer=layer, heads=heads, s_len=s_len, page=page, ppb=ppb)
    per_head = pl.BlockSpec((None, None, s_len, head_dim), lambda i, pt, sl: (i // heads, i % heads, 0, 0))
    grid_spec = pltpu.PrefetchScalarGridSpec(
        num_scalar_prefetch=2,
        grid=(bd * heads,),
        in_specs=[per_head, per_head, per_head,
                  pl.BlockSpec(memory_space=pl.ANY), pl.BlockSpec(memory_space=pl.ANY)],
        out_specs=per_head,
        scratch_shapes=[pltpu.VMEM((2, head_dim, n_cols), _F32),
                        pltpu.VMEM((2, head_dim, n_cols), _F32),
                        pltpu.SemaphoreType.DMA((2, 2))],
    )
    return pl.pallas_call(
        body,
        grid_spec=grid_spec,
        out_shape=jax.ShapeDtypeStruct((bd, heads, s_len, head_dim), _BF16),
        compiler_params=_cparams("arbitrary"),
        name="moba_sample",
    )(page_table, sel, q_h, kn_h, vn_h, cache_kt, cache_vt)


def _ssm_prep_kernel(lr_row, li_row, lr_col, li_col, ldt, br_ref, bi_ref, cr_ref, ci_ref,
                     ca_re, ca_im, bp_re, bp_im, kt_ref, ap_re, ap_im, *, n_lag):
    dt = jnp.exp(ldt[...])

    def discretise(lr, li):
        mag = jnp.exp(lr * dt)
        a_re, a_im = mag * jnp.cos(li * dt), mag * jnp.sin(li * dt)
        den = lr * lr + li * li
        f_re = ((a_re - 1.0) * lr + a_im * li) / den
        f_im = (a_im * lr - (a_re - 1.0) * li) / den
        return a_re, a_im, f_re, f_im

    ar_r, ai_r, _, _ = discretise(lr_row[...], li_row[...])
    ar_c, ai_c, f_re, f_im = discretise(lr_col[...], li_col[...])
    br, bi = br_ref[...], bi_ref[...]
    bb_re = f_re * br - f_im * bi
    bb_im = f_re * bi + f_im * br
    cr, ci = cr_ref[...], ci_ref[...]

    pr_r, pi_r = jnp.ones_like(ar_r), jnp.zeros_like(ar_r)
    pr_c, pi_c = jnp.ones_like(ar_c), jnp.zeros_like(ar_c)
    for lag in range(n_lag + 1):
        car = cr * pr_r - ci * pi_r
        cai = cr * pi_r + ci * pr_r
        ca_re[lag] = car
        ca_im[lag] = cai
        ap_re[lag] = pr_r
        ap_im[lag] = pi_r
        if lag < n_lag:
            bp_re[lag] = pr_c * bb_re - pi_c * bb_im
            bp_im[lag] = pr_c * bb_im + pi_c * bb_re
            kt_ref[lag] = (jnp.dot(car, bb_re, preferred_element_type=_F32, precision=lax.Precision.HIGHEST)
                           - jnp.dot(cai, bb_im, preferred_element_type=_F32, precision=lax.Precision.HIGHEST))
        pr_r, pi_r = pr_r * ar_r - pi_r * ai_r, pr_r * ai_r + pi_r * ar_r
        pr_c, pi_c = pr_c * ar_c - pi_c * ai_c, pr_c * ai_c + pi_c * ar_c


def _ssm_prep_call(lam_re, lam_im, log_dt, b_re, b_im, c_re, c_im):
    g, p = lam_re.shape
    hg = b_re.shape[-1]
    n_lag = SSM_CHUNK
    body = functools.partial(_ssm_prep_kernel, n_lag=n_lag)
    row = pl.BlockSpec((None, 1, p), lambda i: (i, 0, 0))
    col = pl.BlockSpec((None, p, 1), lambda i: (i, 0, 0))

    def full(*dims):
        return pl.BlockSpec((None,) + dims, lambda i: (i,) + (0,) * len(dims))

    return pl.pallas_call(
        body,
        grid=(g,),
        in_specs=[row, row, col, col, full(1, 1), full(p, hg), full(p, hg), full(hg, p), full(hg, p)],
        out_specs=[full(n_lag + 1, hg, p), full(n_lag + 1, hg, p), full(n_lag, p, hg), full(n_lag, p, hg),
                   full(n_lag, hg, hg), full(n_lag + 1, 1, p), full(n_lag + 1, 1, p)],
        out_shape=[jax.ShapeDtypeStruct((g, n_lag + 1, hg, p), _F32),
                   jax.ShapeDtypeStruct((g, n_lag + 1, hg, p), _F32),
                   jax.ShapeDtypeStruct((g, n_lag, p, hg), _F32),
                   jax.ShapeDtypeStruct((g, n_lag, p, hg), _F32),
                   jax.ShapeDtypeStruct((g, n_lag, hg, hg), _F32),
                   jax.ShapeDtypeStruct((g, n_lag + 1, 1, p), _F32),
                   jax.ShapeDtypeStruct((g, n_lag + 1, 1, p), _F32)],
        compiler_params=_cparams("parallel"),
        name="ssm_prep",
    )(lam_re[:, None, :], lam_im[:, None, :], lam_re[:, :, None], lam_im[:, :, None],
      log_dt[:, None, None], b_re, b_im, c_re, c_im)


def _ssm_tables(prep, d_skip, n_tok):
    ca_re, ca_im, bp_re, bp_im, kt, ap_re, ap_im = prep
    g, _, hg, p = ca_re.shape
    tc = SSM_CHUNK
    w = tc * hg
    s_idx = jnp.arange(tc)[:, None]
    t_idx = jnp.arange(tc)[None, :]
    lag = t_idx - s_idx
    toep = jnp.where((lag >= 0)[None, :, :, None, None], kt[:, jnp.clip(lag, 0, tc - 1)], 0.0)
    m_intra = toep.transpose(0, 1, 4, 2, 3).reshape(g, w, w)
    end_lag = n_tok - 1 - jnp.arange(tc)
    live = (end_lag >= 0)[None, :, None, None]
    end_lag = jnp.clip(end_lag, 0, tc - 1)
    bend_re = jnp.where(live, bp_re[:, end_lag], 0.0).transpose(0, 1, 3, 2).reshape(g, w, p)
    bend_im = jnp.where(live, bp_im[:, end_lag], 0.0).transpose(0, 1, 3, 2).reshape(g, w, p)
    cp_re = ca_re[:, 1:tc + 1].transpose(0, 3, 1, 2).reshape(g, p, w)
    cp_im = (-ca_im[:, 1:tc + 1]).transpose(0, 3, 1, 2).reshape(g, p, w)
    dvec = jnp.tile(d_skip[:, None, :], (1, tc, 1)).reshape(g, 1, w)
    return (m_intra.astype(_BF16), bend_re.astype(_BF16), bend_im.astype(_BF16),
            cp_re.astype(_BF16), cp_im.astype(_BF16), ap_re[:, n_tok], ap_im[:, n_tok], dvec)


def _ssm_kernel(x_ref, mt_ref, ber_ref, bei_ref, cpr_ref, cpi_ref, aer_ref, aei_ref, dv_ref,
                h0r_ref, h0i_ref, y_ref, hr_ref, hi_ref, sr, si, hsr, hsi, *, gb, nb, nc):
    for g in range(gb):
        xb = x_ref[g].astype(_BF16)
        sr[g] = _dot(xb, ber_ref[g])
        si[g] = _dot(xb, bei_ref[g])

    a_re, a_im = aer_ref[...], aei_ref[...]
    if nc == 1:
        h_re, h_im = h0r_ref[...], h0i_ref[...]
        hsr[...] = h_re
        hsi[...] = h_im
        hr_ref[...] = a_re * h_re - a_im * h_im + sr[...]
        hi_ref[...] = a_re * h_im + a_im * h_re + si[...]
    else:
        def step(j, carry):
            nxt = []
            for b in range(nb):
                h_re, h_im = carry[b]
                row = pl.ds(b * nc + j, 1)
                hsr[:, row, :] = h_re
                hsi[:, row, :] = h_im
                nxt.append((a_re * h_re - a_im * h_im + sr[:, row, :],
                            a_re * h_im + a_im * h_re + si[:, row, :]))
            return tuple(nxt)

        init = tuple((h0r_ref[:, b:b + 1, :], h0i_ref[:, b:b + 1, :]) for b in range(nb))
        fin = lax.fori_loop(0, nc, step, init)
        for b in range(nb):
            hr_ref[:, b:b + 1, :] = fin[b][0]
            hi_ref[:, b:b + 1, :] = fin[b][1]

    for g in range(gb):
        x = x_ref[g]
        y = _dot(x.astype(_BF16), mt_ref[g])
        y = y + _dot(hsr[g].astype(_BF16), cpr_ref[g]) + _dot(hsi[g].astype(_BF16), cpi_ref[g])
        y_ref[g] = y + x * dv_ref[g]


def _ssm_call(xg, tables, h0_re, h0_im, *, nb, nc, gb):
    m_intra, bend_re, bend_im, cp_re, cp_im, aend_re, aend_im, dvec = tables
    g, n, w = xg.shape
    p = bend_re.shape[-1]
    assert n == nb * nc
    body = functools.partial(_ssm_kernel, gb=gb, nb=nb, nc=nc)

    def spec(*dims):
        return pl.BlockSpec((gb,) + dims, lambda i: (i,) + (0,) * len(dims))

    return pl.pallas_call(
        body,
        grid=(g // gb,),
        in_specs=[spec(n, w), spec(w, w), spec(w, p), spec(w, p), spec(p, w), spec(p, w),
                  spec(1, p), spec(1, p), spec(1, w), spec(nb, p), spec(nb, p)],
        out_specs=[spec(n, w), spec(nb, p), spec(nb, p)],
        out_shape=[jax.ShapeDtypeStruct((g, n, w), _F32),
                   jax.ShapeDtypeStruct((g, nb, p), _F32),
                   jax.ShapeDtypeStruct((g, nb, p), _F32)],
        scratch_shapes=[pltpu.VMEM((gb, n, p), _F32)] * 4,
        compiler_params=_cparams("parallel"),
        name="ssm",
    )(xg, m_intra, bend_re, bend_im, cp_re, cp_im, aend_re, aend_im, dvec, h0_re, h0_im)


def _merge_kernel(x_ref, oa_ref, y_ref, g_ref, wg_ref, wglu_ref, wpa_ref, wpb_ref, wout_ref, fg_ref,
                  o_ref, slab_ref, *, d_attn, d_ssm, hg, tok, final_norm):
    x = x_ref[...]
    d = x.shape[-1]
    hb = _rmsnorm(x, g_ref[...]).astype(_BF16)
    z_a = _dot(hb, wg_ref[:, 0:d_attn])
    ya = oa_ref[...].astype(_F32) * jax.nn.silu(z_a)
    br_a = _dot(ya.astype(_BF16), wpa_ref[...])
    ys = _chunks_to_tokens(y_ref, slab_ref, hg=hg, tok=tok)
    glu = _dot(jax.nn.gelu(ys).astype(_BF16), wglu_ref[...])
    z_b = _dot(hb, wg_ref[:, d_attn:d_attn + d_ssm])
    yb = glu[:, :d_ssm] * jax.nn.sigmoid(glu[:, d_ssm:]) * jax.nn.silu(z_b)
    br_b = _dot(yb.astype(_BF16), wpb_ref[...])
    g_a = _dot(hb, wg_ref[:, d_attn + d_ssm:d_attn + d_ssm + d])
    g_b = _dot(hb, wg_ref[:, d_attn + d_ssm + d:d_attn + d_ssm + 2 * d])
    merged = jax.nn.sigmoid(g_a) * br_a + jax.nn.sigmoid(g_b) * br_b
    out = x + _dot(merged.astype(_BF16), wout_ref[...])
    if final_norm:
        out = _rmsnorm(out, fg_ref[...])
    o_ref[...] = out


def _merge_call(x2d, oa, yg, g, w_g, w_glu, w_pa, w_pb, w_out, final_g, *, hg, tok, tm, final_norm):
    t, d = x2d.shape
    d_attn = oa.shape[1]
    n_groups, _, w = yg.shape
    d_ssm = n_groups * hg
    row = lambda i: (i, 0)
    const = lambda a: pl.BlockSpec(a.shape, lambda i: (0, 0), pipeline_mode=pl.Buffered(1))
    body = functools.partial(_merge_kernel, d_attn=d_attn, d_ssm=d_ssm, hg=hg, tok=tok, final_norm=final_norm)
    return pl.pallas_call(
        body,
        grid=(t // tm,),
        in_specs=[pl.BlockSpec((tm, d), row), pl.BlockSpec((tm, d_attn), row),
                  pl.BlockSpec((n_groups, tm // tok, w), lambda i: (0, i, 0)),
                  const(g), const(w_g), const(w_glu), const(w_pa), const(w_pb), const(w_out), const(final_g)],
        out_specs=pl.BlockSpec((tm, d), row),
        out_shape=jax.ShapeDtypeStruct((t, d), _F32),
        scratch_shapes=[pltpu.VMEM((d_ssm // V7X_LANES, tm, V7X_LANES), _F32)],
        compiler_params=_cparams("parallel"),
        name="merge",
    )(x2d, oa, yg, g, w_g, w_glu, w_pa, w_pb, w_out, final_g)


def _largest_tile(t, cap):
    tm = min(t, cap)
    while t % tm:
        tm //= 2
    return tm


def kernel(x_prompt, x_sample, cache_k, cache_v, state_ssm_re, state_ssm_im, page_table, norm_g, w_in,
           lam_re, lam_im, log_dt, b_re, b_im, c_re, c_im, d_skip, w_glu, w_pa, w_pb, w_out, final_norm_g):
    b, l, d = x_prompt.shape
    bd, s_len, _ = x_sample.shape
    depth = norm_g.shape[0]
    _, _, page, heads, head_dim = cache_k.shape
    n_groups, p_state = lam_re.shape[1:]
    hg = b_re.shape[-1]
    d_attn, d_ssm = heads * head_dim, n_groups * hg
    n_pages = page_table.shape[1]
    nbp = n_pages * page // MOBA_BLOCK
    assert w_in.shape[-1] == 4 * d_attn + 2 * d_ssm + 2 * d
    assert l % MOBA_BLOCK == 0 and MOBA_BLOCK % page == 0 and (n_pages * page) % MOBA_BLOCK == 0
    assert s_len <= SSM_CHUNK and SSM_CHUNK * hg == V7X_MXU_DIM and l % SSM_CHUNK == 0
    assert V7X_LANES % hg == 0 and d_ssm % V7X_LANES == 0 and d_attn % V7X_MXU_DIM == 0

    tm_p = _largest_tile(b * l, 512)
    tm_m = _largest_tile(b * l, 256)
    assert tm_p % MOBA_BLOCK == 0 and l % tm_p == 0 and tm_m % (SSM_CHUNK * V7X_SUBLANES) == 0
    tm_s = bd * s_len
    xp = x_prompt.reshape(b * l, d)
    xs = x_sample.reshape(bd * s_len, d)
    cache_kt = cache_k.transpose(0, 1, 3, 4, 2)
    cache_vt = cache_v.transpose(0, 1, 3, 4, 2)
    zeros_state = jnp.zeros((n_groups, b, p_state), _F32)
    fg = final_norm_g[None, :]
    proj_kw = dict(d_attn=d_attn, d_ssm=d_ssm, head_dim=head_dim, hg=hg)

    kp, vp, ks, vs, hpr, hpi, hsr, hsi = ([] for _ in range(8))
    for layer in range(depth):
        wl = w_in[layer]
        o_z, o_u, o_zb, o_ga = 3 * d_attn, 4 * d_attn, 4 * d_attn + d_ssm, 4 * d_attn + 2 * d_ssm
        w_a = jnp.concatenate([wl[:, :o_z], wl[:, o_u:o_zb]], axis=1).astype(_BF16)
        w_g = jnp.concatenate([wl[:, o_z:o_u], wl[:, o_zb:o_ga], wl[:, o_ga:]], axis=1).astype(_BF16)
        gl = norm_g[layer][None, :]
        merge_w = (gl, w_g, w_glu[layer].astype(_BF16), w_pa[layer].astype(_BF16),
                   w_pb[layer].astype(_BF16), w_out[layer].astype(_BF16), fg)
        last = layer == depth - 1
        prep = _ssm_prep_call(lam_re[layer], lam_im[layer], log_dt[layer], b_re[layer], b_im[layer],
                              c_re[layer], c_im[layer])

        qb, k, v, kb, vb, xg, kmean = _proj_call(xp, gl, w_a, tok=SSM_CHUNK, tm=tm_p, with_kmean=True, **proj_kw)
        oa = _moba_prompt_call(qb.reshape(b, l, d_attn), kb.reshape(b, l, d_attn), vb.reshape(b, l, d_attn),
                               kmean.reshape(b, l // MOBA_BLOCK, d_attn), head_dim=head_dim)
        yg, h_re, h_im = _ssm_call(xg, _ssm_tables(prep, d_skip[layer], SSM_CHUNK), zeros_state, zeros_state,
                                   nb=b, nc=l // SSM_CHUNK, gb=4)
        xp = _merge_call(xp, oa.reshape(b * l, d_attn), yg, *merge_w, hg=hg, tok=SSM_CHUNK, tm=tm_m,
                         final_norm=last)
        kp.append(k.reshape(b, l, heads, head_dim))
        vp.append(v.reshape(b, l, heads, head_dim))
        hpr.append(h_re.transpose(1, 0, 2))
        hpi.append(h_im.transpose(1, 0, 2))

        qb, k, v, kb, vb, xg = _proj_call(xs, gl, w_a, tok=s_len, tm=tm_s, with_kmean=False, **proj_kw)
        kmean_t = _kmean_pages_call(cache_kt, page_table, layer, pages_per_step=min(16, n_pages))
        sel = _select_call(qb.reshape(bd, s_len, d_attn), kmean_t.reshape(bd, d_attn, V7X_LANES),
                           heads=heads, head_dim=head_dim, nbp=nbp)
        sel = sel[:, :, :MOBA_TOPK].reshape(-1)
        to_heads = lambda a: a.reshape(bd, s_len, heads, head_dim).transpose(0, 2, 1, 3)
        oa = _sample_attend_call(page_table, sel, to_heads(qb), to_heads(kb), to_heads(vb),
                                 cache_kt, cache_vt, layer)
        oa = oa.transpose(0, 2, 1, 3).reshape(bd * s_len, d_attn)
        yg, h_re, h_im = _ssm_call(xg, _ssm_tables(prep, d_skip[layer], s_len),
                                   state_ssm_re[layer].transpose(1, 0, 2), state_ssm_im[layer].transpose(1, 0, 2),
                                   nb=bd, nc=1, gb=4)
        xs = _merge_call(xs, oa, yg, *merge_w, hg=hg, tok=s_len, tm=tm_s, final_norm=last)
        ks.append(k.reshape(bd, s_len, heads, head_dim))
        vs.append(v.reshape(bd, s_len, heads, head_dim))
        hsr.append(h_re.transpose(1, 0, 2))
        hsi.append(h_im.transpose(1, 0, 2))

    return (xp.reshape(b, l, d), xs.reshape(bd, s_len, d),
            jnp.stack(kp), jnp.stack(vp), jnp.stack(ks), jnp.stack(vs),
            jnp.stack(hpr), jnp.stack(hpi), jnp.stack(hsr), jnp.stack(hsi))
```

```python
import functools

import jax
import jax.numpy as jnp
from jax import lax
from jax.experimental import pallas as pl
from jax.experimental.pallas import tpu as pltpu

MOBA_BLOCK = 256
MOBA_TOPK = 3
RMS_EPS = 1e-6

V7X_LANES = 128
V7X_SUBLANES = 8
V7X_MXU_DIM = 256
V7X_VMEM_LIMIT_BYTES = 56 * 1024 * 1024

SSM_CHUNK = 16
MASKED_LOGIT = -1e30
LOG2_E = 1.4426950408889634

_F32 = jnp.float32
_BF16 = jnp.bfloat16


def _cparams(*sem):
    return pltpu.CompilerParams(dimension_semantics=sem, vmem_limit_bytes=V7X_VMEM_LIMIT_BYTES)


def _rmsnorm(x, g):
    return x * lax.rsqrt(jnp.mean(x * x, axis=-1, keepdims=True) + RMS_EPS) * g


def _dot(a, b):
    return jnp.dot(a, b, preferred_element_type=_F32)


def _dot_nt(a, b):
    return lax.dot_general(a, b, (((1,), (1,)), ((), ())), preferred_element_type=_F32)


def _twice(a):
    return jnp.concatenate([a, a], axis=1)


def _tokens_to_chunks(u, slab_ref, xg_ref, *, hg, tok):
    tm, c = u.shape
    n = tm // tok
    gpl = V7X_LANES // hg
    lane_slot = lax.broadcasted_iota(jnp.int32, (n, V7X_LANES), 1) // hg
    for cb in range(c // V7X_LANES):
        slab_ref[cb] = u[:, cb * V7X_LANES:(cb + 1) * V7X_LANES]
    for cb in range(c // V7X_LANES):
        rows = [slab_ref[cb, pl.ds(t, n, stride=tok), :] for t in range(tok)]
        rolled = {}
        for gi in range(gpl):
            cols = []
            for cv in range(SSM_CHUNK // gpl):
                acc = jnp.zeros((n, V7X_LANES), _F32)
                for jj in range(gpl):
                    t = cv * gpl + jj
                    if t >= tok:
                        continue
                    shift = ((jj - gi) * hg) % V7X_LANES
                    if (t, shift) not in rolled:
                        rolled[(t, shift)] = pltpu.roll(rows[t], shift, 1) if shift else rows[t]
                    acc = jnp.where(lane_slot == jj, rolled[(t, shift)], acc)
                cols.append(acc)
            xg_ref[cb * gpl + gi] = jnp.concatenate(cols, axis=1)


def _chunks_to_tokens(y_ref, slab_ref, *, hg, tok):
    g, n, _ = y_ref.shape
    gpl = V7X_LANES // hg
    lane_slot = lax.broadcasted_iota(jnp.int32, (n, V7X_LANES), 1) // hg
    n_slabs = g // gpl
    for cb in range(n_slabs):
        for cv in range(SSM_CHUNK // gpl):
            src = [y_ref[cb * gpl + gi, :, cv * V7X_LANES:(cv + 1) * V7X_LANES] for gi in range(gpl)]
            rolled = {}
            for jj in range(gpl):
                t = cv * gpl + jj
                if t >= tok:
                    continue
                piece = jnp.zeros((n, V7X_LANES), _F32)
                for gi in range(gpl):
                    shift = ((gi - jj) * hg) % V7X_LANES
                    if (gi, shift) not in rolled:
                        rolled[(gi, shift)] = pltpu.roll(src[gi], shift, 1) if shift else src[gi]
                    piece = jnp.where(lane_slot == gi, rolled[(gi, shift)], piece)
                slab_ref[cb, pl.ds(t, n, stride=tok), :] = piece
    return jnp.concatenate([slab_ref[cb] for cb in range(n_slabs)], axis=1)


def _proj_kernel(x_ref, g_ref, w_ref, q_ref, k_ref, v_ref, kb_ref, vb_ref, xg_ref, km_ref, slab_ref,
                 *, d_attn, d_ssm, head_dim, blk, hg, tok, kv_token_minor):
    hb = _rmsnorm(x_ref[...], g_ref[...]).astype(_BF16)
    q = _dot(hb, w_ref[:, 0:d_attn])
    k = _dot(hb, w_ref[:, d_attn:2 * d_attn])
    v = _dot(hb, w_ref[:, 2 * d_attn:3 * d_attn])
    u = _dot(hb, w_ref[:, 3 * d_attn:3 * d_attn + d_ssm])
    q_ref[...] = (q * (head_dim ** -0.5 * LOG2_E)).astype(_BF16)
    k_ref[...] = k.T if kv_token_minor else k
    v_ref[...] = v.T if kv_token_minor else v
    kb_ref[...] = k.astype(_BF16)
    vb_ref[...] = v.astype(_BF16)
    _tokens_to_chunks(u, slab_ref, xg_ref, hg=hg, tok=tok)
    if km_ref is not None:
        tm = k.shape[0]
        km_ref[0] = jnp.sum(k.reshape(tm // blk, blk, d_attn), axis=1) * (1.0 / blk)


def _proj_call(x2d, g, w_a, *, d_attn, d_ssm, head_dim, hg, tok, tm, with_kmean, seq_len=None):
    t, d = x2d.shape
    n_tiles = t // tm
    n_groups = d_ssm // hg
    row = lambda i: (i, 0)
    if seq_len is None:
        kv_shape = jax.ShapeDtypeStruct((t, d_attn), _F32)
        kv_spec = pl.BlockSpec((tm, d_attn), row)
    else:
        tps = seq_len // tm
        kv_shape = jax.ShapeDtypeStruct((t // seq_len, d_attn, seq_len), _F32)
        kv_spec = pl.BlockSpec((None, d_attn, tm), lambda i: (i // tps, 0, i % tps))
    out_shape = [
        jax.ShapeDtypeStruct((t, d_attn), _BF16),
        kv_shape,
        kv_shape,
        jax.ShapeDtypeStruct((t, d_attn), _BF16),
        jax.ShapeDtypeStruct((t, d_attn), _BF16),
        jax.ShapeDtypeStruct((n_groups, t // tok, SSM_CHUNK * hg), _F32),
    ]
    out_specs = [pl.BlockSpec((tm, d_attn), row), kv_spec, kv_spec,
                 pl.BlockSpec((tm, d_attn), row), pl.BlockSpec((tm, d_attn), row),
                 pl.BlockSpec((n_groups, tm // tok, SSM_CHUNK * hg), lambda i: (0, i, 0))]
    body = functools.partial(_proj_kernel, d_attn=d_attn, d_ssm=d_ssm, head_dim=head_dim, blk=MOBA_BLOCK,
                             hg=hg, tok=tok, kv_token_minor=seq_len is not None)
    if with_kmean:
        out_shape.append(jax.ShapeDtypeStruct((n_tiles, tm // MOBA_BLOCK, d_attn), _F32))
        out_specs.append(pl.BlockSpec((1, tm // MOBA_BLOCK, d_attn), lambda i: (i, 0, 0)))
        kern = body
    else:
        kern = lambda *refs: body(*refs[:-1], None, refs[-1])
    return pl.pallas_call(
        kern,
        grid=(n_tiles,),
        in_specs=[pl.BlockSpec((tm, d), row),
                  pl.BlockSpec((1, d), lambda i: (0, 0)),
                  pl.BlockSpec(w_a.shape, lambda i: (0, 0))],
        out_specs=out_specs,
        out_shape=out_shape,
        scratch_shapes=[pltpu.VMEM((d_ssm // V7X_LANES, tm, V7X_LANES), _F32)],
        compiler_params=_cparams("parallel"),
        name="proj",
    )(x2d, g, w_a)


def _moba_prompt_kernel(q_ref, k_ref, v_ref, km_ref, o_ref, kaug_ref, m_ref, l_ref, acc_ref,
                        *, nb, head_dim, blk):
    i = pl.program_id(2)
    width = q_ref.shape[-1]
    pw = V7X_LANES
    n_pairs = width // pw
    hpp = pw // head_dim
    prow = hpp * blk
    rows = n_pairs * prow
    seq = k_ref.shape[0]

    @pl.when(i == 0)
    def _():
        key_blk = lax.broadcasted_iota(jnp.int32, (seq, pw), 0) // blk
        onehot = jnp.where(key_blk == lax.broadcasted_iota(jnp.int32, (seq, pw), 1), 1.0, 0.0).astype(_BF16)
        for p in range(n_pairs):
            kaug_ref[p, :, 0:pw] = k_ref[:, p * pw:(p + 1) * pw]
            kaug_ref[p, :, pw:2 * pw] = onehot

    q = q_ref[...]
    lane_head = lax.broadcasted_iota(jnp.int32, (blk, pw), 1) // head_dim
    q2 = []
    for p in range(n_pairs):
        qp = q[:, p * pw:(p + 1) * pw]
        q2.append(jnp.concatenate([jnp.where(lane_head == h, qp, jnp.zeros_like(qp)) for h in range(hpp)], axis=0))

    km = km_ref[...].astype(_BF16)
    gate = jnp.concatenate([_dot_nt(km[:, p * pw:(p + 1) * pw], q2[p]) for p in range(n_pairs)], axis=1)
    n_iota = lax.broadcasted_iota(jnp.int32, (nb, rows), 0)
    gate = jnp.where(n_iota < i, gate, -jnp.inf)
    rank = jnp.zeros((nb, rows), jnp.int32)
    for m in range(nb):
        gm = gate[m:m + 1, :]
        tie_first = jnp.where(m < n_iota, 1, 0)
        rank = rank + jnp.where(gm > gate, 1, jnp.where(gm == gate, tie_first, 0))
    keep = jnp.where(n_iota < i, jnp.where(rank < MOBA_TOPK, 1, 0), jnp.where(n_iota == i, 1, 0))
    bias = jnp.where(keep == 1, 0.0, MASKED_LOGIT).astype(_F32)
    bias = jnp.concatenate([bias, jnp.full((pw - nb, rows), MASKED_LOGIT, _F32)], axis=0).T
    qaug = [jnp.concatenate([q2[p], bias[p * prow:(p + 1) * prow].astype(_BF16)], axis=1) for p in range(n_pairs)]

    span = 2 * blk
    n_chunks = span // pw

    def logits(kstart):
        return jnp.concatenate([_dot_nt(qaug[p], kaug_ref[p, pl.ds(kstart, span), :]) for p in range(n_pairs)],
                               axis=0)

    def lane_chunks(a):
        return [a[:, c * pw:(c + 1) * pw] for c in range(n_chunks)]

    def row_max(s):
        return jnp.broadcast_to(jnp.max(functools.reduce(jnp.maximum, lane_chunks(s)), axis=1, keepdims=True),
                                (rows, pw))

    def widen(a, n):
        return jnp.concatenate([a] * n, axis=1)

    own_span = i // 2
    start = pl.multiple_of(own_span * span, span)
    s = logits(start)
    q_off = lax.broadcasted_iota(jnp.int32, (rows, span), 0) % blk + (i - 2 * own_span) * blk
    k_off = lax.broadcasted_iota(jnp.int32, (rows, span), 1)
    s = jnp.where(k_off <= q_off, s, MASKED_LOGIT)
    m0 = row_max(s)
    p0 = jnp.exp2(s - widen(m0, n_chunks))
    m_ref[...] = m0
    l_ref[...] = functools.reduce(jnp.add, lane_chunks(p0))
    acc_ref[...] = _dot(p0.astype(_BF16), v_ref[pl.ds(start, span), :])

    for n in range(nb // 2 - 1):
        @pl.when(n < own_span)
        def _(n=n):
            s = logits(n * span)
            m_old = m_ref[...]
            m_new = jnp.maximum(m_old, row_max(s))
            alpha = jnp.exp2(m_old - m_new)
            p = jnp.exp2(s - widen(m_new, n_chunks))
            l_ref[...] = alpha * l_ref[...] + functools.reduce(jnp.add, lane_chunks(p))
            acc_ref[...] = _twice(alpha) * acc_ref[...] + _dot(p.astype(_BF16), v_ref[n * span:(n + 1) * span, :])
            m_ref[...] = m_new

    o4 = acc_ref[...] / jnp.sum(l_ref[...], axis=1, keepdims=True)
    lane_head4 = lax.broadcasted_iota(jnp.int32, (blk, width), 1) // head_dim
    out = jnp.zeros((blk, width), _F32)
    for h in range(width // head_dim):
        out = out + jnp.where(lane_head4 == h, o4[h * blk:(h + 1) * blk, :], 0.0)
    o_ref[...] = out.astype(o_ref.dtype)


def _moba_prompt_call(qb, kb, vb, kmean, *, head_dim):
    b, l, d_attn = qb.shape
    blk = MOBA_BLOCK
    nb = l // blk
    assert nb % 2 == 0
    width = V7X_MXU_DIM
    rows = (width // head_dim) * blk
    body = functools.partial(_moba_prompt_kernel, nb=nb, head_dim=head_dim, blk=blk)
    return pl.pallas_call(
        body,
        grid=(b, d_attn // width, nb),
        in_specs=[pl.BlockSpec((None, blk, width), lambda bi, c, i: (bi, i, c)),
                  pl.BlockSpec((None, l, width), lambda bi, c, i: (bi, 0, c)),
                  pl.BlockSpec((None, l, width), lambda bi, c, i: (bi, 0, c)),
                  pl.BlockSpec((None, nb, width), lambda bi, c, i: (bi, 0, c))],
        out_specs=pl.BlockSpec((None, blk, width), lambda bi, c, i: (bi, i, c)),
        out_shape=jax.ShapeDtypeStruct((b, l, d_attn), _BF16),
        scratch_shapes=[pltpu.VMEM((width // V7X_LANES, l, 2 * V7X_LANES), _BF16),
                        pltpu.VMEM((rows, V7X_LANES), _F32),
                        pltpu.VMEM((rows, V7X_LANES), _F32),
                        pltpu.VMEM((rows, width), _F32)],
        compiler_params=_cparams("parallel", "parallel", "arbitrary"),
        name="moba_prompt",
    )(qb, kb, vb, kmean)


def _kmean_pages_kernel(pt_ref, ck_ref, out_ref, buf, sem, *, layer, pages_per_step, pages_per_block, steps,
                        inv_blk):
    i = pl.program_id(0)
    n_total = pl.num_programs(0)
    c = i % steps
    bps = pages_per_step // pages_per_block

    def copies(step, slot):
        b = step // steps
        first = (step % steps) * pages_per_step
        return [pltpu.make_async_copy(ck_ref.at[layer, pt_ref[b, first + r]], buf.at[slot, r], sem.at[slot])
                for r in range(pages_per_step)]

    def reduce_pages(slot):
        lane = lax.broadcasted_iota(jnp.int32, out_ref.shape, 2)
        acc = out_ref[...]
        for j in range(bps):
            tot = buf[slot, j * pages_per_block]
            for r in range(1, pages_per_block):
                tot = tot + buf[slot, j * pages_per_block + r]
            mean = jnp.sum(tot, axis=-1, keepdims=True) * inv_blk
            acc = jnp.where(lane == c * bps + j, mean, acc)
        out_ref[...] = acc

    @pl.when(i == 0)
    def _():
        for cp in copies(0, 0):
            cp.start()

    @pl.when(c == 0)
    def _():
        out_ref[...] = jnp.zeros_like(out_ref)

    for parity in range(2):
        @pl.when(i % 2 == parity)
        def _(parity=parity):
            @pl.when(i + 1 < n_total)
            def _():
                for cp in copies(i + 1, 1 - parity):
                    cp.start()
            for cp in copies(i, parity):
                cp.wait()
            reduce_pages(parity)


def _kmean_pages_call(cache_kt, page_table, layer, *, pages_per_step):
    _, _, heads, head_dim, page = cache_kt.shape
    bd, n_pages = page_table.shape
    ppb = MOBA_BLOCK // page
    assert n_pages // ppb <= V7X_LANES and n_pages % pages_per_step == 0 and pages_per_step % ppb == 0
    steps = n_pages // pages_per_step
    grid_spec = pltpu.PrefetchScalarGridSpec(
        num_scalar_prefetch=1,
        grid=(bd * steps,),
        in_specs=[pl.BlockSpec(memory_space=pl.ANY)],
        out_specs=pl.BlockSpec((None, heads, head_dim, V7X_LANES), lambda i, pt: (i // steps, 0, 0, 0)),
        scratch_shapes=[pltpu.VMEM((2, pages_per_step, heads, head_dim, page), _F32),
                        pltpu.SemaphoreType.DMA((2,))],
    )
    body = functools.partial(_kmean_pages_kernel, layer=layer, pages_per_step=pages_per_step,
                             pages_per_block=ppb, steps=steps, inv_blk=1.0 / MOBA_BLOCK)
    return pl.pallas_call(
        body,
        grid_spec=grid_spec,
        out_shape=jax.ShapeDtypeStruct((bd, heads, head_dim, V7X_LANES), _F32),
        compiler_params=_cparams("arbitrary"),
        name="kmean_pages",
    )(page_table, cache_kt)


def _select_kernel(qh_ref, kmt_ref, sel_ref, *, nbp):
    rows = qh_ref.shape[0]
    gate = _dot(qh_ref[...], kmt_ref[...].astype(_BF16))
    n_iota = lax.broadcasted_iota(jnp.int32, (rows, V7X_LANES), 1)
    gate = jnp.where(n_iota < nbp, gate, -jnp.inf)
    rank = jnp.zeros((rows, V7X_LANES), jnp.int32)
    for m in range(nbp):
        gm = gate[:, m:m + 1]
        tie_first = jnp.where(m < n_iota, 1, 0)
        rank = rank + jnp.where(gm > gate, 1, jnp.where(gm == gate, tie_first, 0))
    out = jnp.zeros((rows, V7X_LANES), jnp.int32)
    for j in range(MOBA_TOPK):
        hit = jnp.where(n_iota < nbp, jnp.where(rank == j, n_iota, 0), 0)
        out = jnp.where(n_iota == j, jnp.sum(hit, axis=1, keepdims=True), out)
    sel_ref[...] = out


def _select_call(q_s, kmean_t, *, heads, head_dim, nbp):
    bd, s_len, d_attn = q_s.shape
    rows = heads * s_len
    head_of_lane = jnp.arange(d_attn) // head_dim
    qh = jnp.where(head_of_lane[None, None, None, :] == jnp.arange(heads)[None, :, None, None],
                   q_s[:, None, :, :], jnp.zeros((), q_s.dtype)).reshape(bd, rows, d_attn)
    return pl.pallas_call(
        functools.partial(_select_kernel, nbp=nbp),
        grid=(bd,),
        in_specs=[pl.BlockSpec((None, rows, d_attn), lambda b: (b, 0, 0)),
                  pl.BlockSpec((None, d_attn, V7X_LANES), lambda b: (b, 0, 0))],
        out_specs=pl.BlockSpec((None, rows, V7X_LANES), lambda b: (b, 0, 0)),
        out_shape=jax.ShapeDtypeStruct((bd, rows, V7X_LANES), jnp.int32),
        compiler_params=_cparams("parallel"),
        name="moba_select",
    )(qh, kmean_t)


def _sample_attend_kernel(pt_ref, sel_ref, q_ref, kn_ref, vn_ref, ck_ref, cv_ref, o_ref, kbuf, vbuf, sem,
                          *, layer, heads, s_len, page, ppb):
    i = pl.program_id(0)
    n_steps = pl.num_programs(0)
    n_sel = s_len * MOBA_TOPK
    blk = page * ppb

    def copies(step, buf):
        b = step // heads
        h = step % heads
        out = []
        for slot in range(n_sel):
            blk_idx = sel_ref[step * n_sel + slot]
            for r in range(ppb):
                pg = pt_ref[b, blk_idx * ppb + r]
                dst = pl.ds((slot * ppb + r) * page, page)
                out.append(pltpu.make_async_copy(ck_ref.at[layer, pg, h], kbuf.at[buf, :, dst], sem.at[0, buf]))
                out.append(pltpu.make_async_copy(cv_ref.at[layer, pg, h], vbuf.at[buf, :, dst], sem.at[1, buf]))
        return out

    def attend(buf):
        q = q_ref[...]
        s_past = _dot(q, kbuf[buf].astype(_BF16))
        qi = lax.broadcasted_iota(jnp.int32, s_past.shape, 0)
        slot_q = lax.broadcasted_iota(jnp.int32, s_past.shape, 1) // (blk * MOBA_TOPK)
        s_past = jnp.where(qi == slot_q, s_past, MASKED_LOGIT)
        s_own = _dot_nt(q, kn_ref[...])
        causal = (lax.broadcasted_iota(jnp.int32, s_own.shape, 1)
                  <= lax.broadcasted_iota(jnp.int32, s_own.shape, 0))
        s_own = jnp.where(causal, s_own, MASKED_LOGIT)
        m = jnp.maximum(jnp.max(s_past, axis=1, keepdims=True), jnp.max(s_own, axis=1, keepdims=True))
        p_past = jnp.exp2(s_past - m)
        p_own = jnp.exp2(s_own - m)
        denom = jnp.sum(p_past, axis=1, keepdims=True) + jnp.sum(p_own, axis=1, keepdims=True)
        o = _dot_nt(p_past.astype(_BF16), vbuf[buf].astype(_BF16)) + _dot(p_own.astype(_BF16), vn_ref[...])
        o_ref[...] = (o / denom).astype(o_ref.dtype)

    @pl.when(i == 0)
    def _():
        for cp in copies(0, 0):
            cp.start()

    for parity in range(2):
        @pl.when(i % 2 == parity)
        def _(parity=parity):
            @pl.when(i + 1 < n_steps)
            def _():
                for cp in copies(i + 1, 1 - parity):
                    cp.start()
            for cp in copies(i, parity):
                cp.wait()
            attend(parity)


def _sample_attend_call(page_table, sel, q_h, kn_h, vn_h, cache_kt, cache_vt, layer):
    bd, heads, s_len, head_dim = q_h.shape
    page = cache_kt.shape[-1]
    ppb = MOBA_BLOCK // page
    n_cols = s_len * MOBA_TOPK * MOBA_BLOCK
    body = functools.partial(_sample_attend_kernel, layer=layer, heads=heads, s_len=s_len, page=page, ppb=ppb)
    per_head = pl.BlockSpec((None, None, s_len, head_dim), lambda i, pt, sl: (i // heads, i % heads, 0, 0))
    grid_spec = pltpu.PrefetchScalarGridSpec(
        num_scalar_prefetch=2,
        grid=(bd * heads,),
        in_specs=[per_head, per_head, per_head,
                  pl.BlockSpec(memory_space=pl.ANY), pl.BlockSpec(memory_space=pl.ANY)],
        out_specs=per_head,
        scratch_shapes=[pltpu.VMEM((2, head_dim, n_cols), _F32),
                        pltpu.VMEM((2, head_dim, n_cols), _F32),
                        pltpu.SemaphoreType.DMA((2, 2))],
    )
    return pl.pallas_call(
        body,
        grid_spec=grid_spec,
        out_shape=jax.ShapeDtypeStruct((bd, heads, s_len, head_dim), _BF16),
        compiler_params=_cparams("arbitrary"),
        name="moba_sample",
    )(page_table, sel, q_h, kn_h, vn_h, cache_kt, cache_vt)


def _ssm_prep_kernel(lr_row, li_row, lr_col, li_col, ldt, br_ref, bi_ref, cr_ref, ci_ref,
                     ca_re, ca_im, bp_re, bp_im, kt_ref, ap_re, ap_im, *, n_lag):
    dt = jnp.exp(ldt[...])

    def discretise(lr, li):
        mag = jnp.exp(lr * dt)
        a_re, a_im = mag * jnp.cos(li * dt), mag * jnp.sin(li * dt)
        den = lr * lr + li * li
        f_re = ((a_re - 1.0) * lr + a_im * li) / den
        f_im = (a_im * lr - (a_re - 1.0) * li) / den
        return a_re, a_im, f_re, f_im

    ar_r, ai_r, _, _ = discretise(lr_row[...], li_row[...])
    ar_c, ai_c, f_re, f_im = discretise(lr_col[...], li_col[...])
    br, bi = br_ref[...], bi_ref[...]
    bb_re = f_re * br - f_im * bi
    bb_im = f_re * bi + f_im * br
    cr, ci = cr_ref[...], ci_ref[...]

    pr_r, pi_r = jnp.ones_like(ar_r), jnp.zeros_like(ar_r)
    pr_c, pi_c = jnp.ones_like(ar_c), jnp.zeros_like(ar_c)
    for lag in range(n_lag + 1):
        car = cr * pr_r - ci * pi_r
        cai = cr * pi_r + ci * pr_r
        ca_re[lag] = car
        ca_im[lag] = cai
        ap_re[lag] = pr_r
        ap_im[lag] = pi_r
        if lag < n_lag:
            bp_re[lag] = pr_c * bb_re - pi_c * bb_im
            bp_im[lag] = pr_c * bb_im + pi_c * bb_re
            kt_ref[lag] = (jnp.dot(car, bb_re, preferred_element_type=_F32, precision=lax.Precision.HIGHEST)
                           - jnp.dot(cai, bb_im, preferred_element_type=_F32, precision=lax.Precision.HIGHEST))
        pr_r, pi_r = pr_r * ar_r - pi_r * ai_r, pr_r * ai_r + pi_r * ar_r
        pr_c, pi_c = pr_c * ar_c - pi_c * ai_c, pr_c * ai_c + pi_c * ar_c


def _ssm_prep_call(lam_re, lam_im, log_dt, b_re, b_im, c_re, c_im):
    g, p = lam_re.shape
    hg = b_re.shape[-1]
    n_lag = SSM_CHUNK
    body = functools.partial(_ssm_prep_kernel, n_lag=n_lag)
    row = pl.BlockSpec((None, 1, p), lambda i: (i, 0, 0))
    col = pl.BlockSpec((None, p, 1), lambda i: (i, 0, 0))

    def full(*dims):
        return pl.BlockSpec((None,) + dims, lambda i: (i,) + (0,) * len(dims))

    return pl.pallas_call(
        body,
        grid=(g,),
        in_specs=[row, row, col, col, full(1, 1), full(p, hg), full(p, hg), full(hg, p), full(hg, p)],
        out_specs=[full(n_lag + 1, hg, p), full(n_lag + 1, hg, p), full(n_lag, p, hg), full(n_lag, p, hg),
                   full(n_lag, hg, hg), full(n_lag + 1, 1, p), full(n_lag + 1, 1, p)],
        out_shape=[jax.ShapeDtypeStruct((g, n_lag + 1, hg, p), _F32),
                   jax.ShapeDtypeStruct((g, n_lag + 1, hg, p), _F32),
                   jax.ShapeDtypeStruct((g, n_lag, p, hg), _F32),
                   jax.ShapeDtypeStruct((g, n_lag, p, hg), _F32),
                   jax.ShapeDtypeStruct((g, n_lag, hg, hg), _F32),
                   jax.ShapeDtypeStruct((g, n_lag + 1, 1, p), _F32),
                   jax.ShapeDtypeStruct((g, n_lag + 1, 1, p), _F32)],
        compiler_params=_cparams("parallel"),
        name="ssm_prep",
    )(lam_re[:, None, :], lam_im[:, None, :], lam_re[:, :, None], lam_im[:, :, None],
      log_dt[:, None, None], b_re, b_im, c_re, c_im)


def _ssm_tables(prep, d_skip, n_tok):
    ca_re, ca_im, bp_re, bp_im, kt, ap_re, ap_im = prep
    g, _, hg, p = ca_re.shape
    tc = SSM_CHUNK
    w = tc * hg
    s_idx = jnp.arange(tc)[:, None]
    t_idx = jnp.arange(tc)[None, :]
    lag = t_idx - s_idx
    toep = jnp.where((lag >= 0)[None, :, :, None, None], kt[:, jnp.clip(lag, 0, tc - 1)], 0.0)
    m_intra = toep.transpose(0, 1, 4, 2, 3).reshape(g, w, w)
    end_lag = n_tok - 1 - jnp.arange(tc)
    live = (end_lag >= 0)[None, :, None, None]
    end_lag = jnp.clip(end_lag, 0, tc - 1)
    bend_re = jnp.where(live, bp_re[:, end_lag], 0.0).transpose(0, 1, 3, 2).reshape(g, w, p)
    bend_im = jnp.where(live, bp_im[:, end_lag], 0.0).transpose(0, 1, 3, 2).reshape(g, w, p)
    cp_re = ca_re[:, 1:tc + 1].transpose(0, 3, 1, 2).reshape(g, p, w)
    cp_im = (-ca_im[:, 1:tc + 1]).transpose(0, 3, 1, 2).reshape(g, p, w)
    dvec = jnp.tile(d_skip[:, None, :], (1, tc, 1)).reshape(g, 1, w)
    return (m_intra.astype(_BF16), bend_re.astype(_BF16), bend_im.astype(_BF16),
            cp_re.astype(_BF16), cp_im.astype(_BF16), ap_re[:, n_tok], ap_im[:, n_tok], dvec)


def _ssm_kernel(x_ref, mt_ref, ber_ref, bei_ref, cpr_ref, cpi_ref, aer_ref, aei_ref, dv_ref,
                h0r_ref, h0i_ref, y_ref, hr_ref, hi_ref, sr, si, hsr, hsi, *, gb, nb, nc):
    for g in range(gb):
        xb = x_ref[g].astype(_BF16)
        sr[g] = _dot(xb, ber_ref[g])
        si[g] = _dot(xb, bei_ref[g])

    a_re, a_im = aer_ref[...], aei_ref[...]
    if nc == 1:
        h_re, h_im = h0r_ref[...], h0i_ref[...]
        hsr[...] = h_re
        hsi[...] = h_im
        hr_ref[...] = a_re * h_re - a_im * h_im + sr[...]
        hi_ref[...] = a_re * h_im + a_im * h_re + si[...]
    else:
        def step(j, carry):
            nxt = []
            for b in range(nb):
                h_re, h_im = carry[b]
                row = pl.ds(b * nc + j, 1)
                hsr[:, row, :] = h_re
                hsi[:, row, :] = h_im
                nxt.append((a_re * h_re - a_im * h_im + sr[:, row, :],
                            a_re * h_im + a_im * h_re + si[:, row, :]))
            return tuple(nxt)

        init = tuple((h0r_ref[:, b:b + 1, :], h0i_ref[:, b:b + 1, :]) for b in range(nb))
        fin = lax.fori_loop(0, nc, step, init)
        for b in range(nb):
            hr_ref[:, b:b + 1, :] = fin[b][0]
            hi_ref[:, b:b + 1, :] = fin[b][1]

    for g in range(gb):
        x = x_ref[g]
        y = _dot(x.astype(_BF16), mt_ref[g])
        y = y + _dot(hsr[g].astype(_BF16), cpr_ref[g]) + _dot(hsi[g].astype(_BF16), cpi_ref[g])
        y_ref[g] = y + x * dv_ref[g]


def _ssm_call(xg, tables, h0_re, h0_im, *, nb, nc, gb):
    m_intra, bend_re, bend_im, cp_re, cp_im, aend_re, aend_im, dvec = tables
    g, n, w = xg.shape
    p = bend_re.shape[-1]
    assert n == nb * nc
    body = functools.partial(_ssm_kernel, gb=gb, nb=nb, nc=nc)

    def spec(*dims):
        return pl.BlockSpec((gb,) + dims, lambda i: (i,) + (0,) * len(dims))

    return pl.pallas_call(
        body,
        grid=(g // gb,),
        in_specs=[spec(n, w), spec(w, w), spec(w, p), spec(w, p), spec(p, w), spec(p, w),
                  spec(1, p), spec(1, p), spec(1, w), spec(nb, p), spec(nb, p)],
        out_specs=[spec(n, w), spec(nb, p), spec(nb, p)],
        out_shape=[jax.ShapeDtypeStruct((g, n, w), _F32),
                   jax.ShapeDtypeStruct((g, nb, p), _F32),
                   jax.ShapeDtypeStruct((g, nb, p), _F32)],
        scratch_shapes=[pltpu.VMEM((gb, n, p), _F32)] * 4,
        compiler_params=_cparams("parallel"),
        name="ssm",
    )(xg, m_intra, bend_re, bend_im, cp_re, cp_im, aend_re, aend_im, dvec, h0_re, h0_im)


def _merge_kernel(x_ref, oa_ref, y_ref, g_ref, wg_ref, wglu_ref, wpa_ref, wpb_ref, wout_ref, fg_ref,
                  o_ref, slab_ref, *, d_attn, d_ssm, hg, tok, final_norm):
    x = x_ref[...]
    d = x.shape[-1]
    hb = _rmsnorm(x, g_ref[...]).astype(_BF16)
    z_a = _dot(hb, wg_ref[:, 0:d_attn])
    ya = oa_ref[...].astype(_F32) * jax.nn.silu(z_a)
    br_a = _dot(ya.astype(_BF16), wpa_ref[...])
    ys = _chunks_to_tokens(y_ref, slab_ref, hg=hg, tok=tok)
    glu = _dot(jax.nn.gelu(ys).astype(_BF16), wglu_ref[...])
    z_b = _dot(hb, wg_ref[:, d_attn:d_attn + d_ssm])
    yb = glu[:, :d_ssm] * jax.nn.sigmoid(glu[:, d_ssm:]) * jax.nn.silu(z_b)
    br_b = _dot(yb.astype(_BF16), wpb_ref[...])
    g_a = _dot(hb, wg_ref[:, d_attn + d_ssm:d_attn + d_ssm + d])
    g_b = _dot(hb, wg_ref[:, d_attn + d_ssm + d:d_attn + d_ssm + 2 * d])
    merged = jax.nn.sigmoid(g_a) * br_a + jax.nn.sigmoid(g_b) * br_b
    out = x + _dot(merged.astype(_BF16), wout_ref[...])
    if final_norm:
        out = _rmsnorm(out, fg_ref[...])
    o_ref[...] = out


def _merge_call(x2d, oa, yg, g, w_g, w_glu, w_pa, w_pb, w_out, final_g, *, hg, tok, tm, final_norm):
    t, d = x2d.shape
    d_attn = oa.shape[1]
    n_groups, _, w = yg.shape
    d_ssm = n_groups * hg
    row = lambda i: (i, 0)
    const = lambda a: pl.BlockSpec(a.shape, lambda i: (0, 0), pipeline_mode=pl.Buffered(1))
    body = functools.partial(_merge_kernel, d_attn=d_attn, d_ssm=d_ssm, hg=hg, tok=tok, final_norm=final_norm)
    return pl.pallas_call(
        body,
        grid=(t // tm,),
        in_specs=[pl.BlockSpec((tm, d), row), pl.BlockSpec((tm, d_attn), row),
                  pl.BlockSpec((n_groups, tm // tok, w), lambda i: (0, i, 0)),
                  const(g), const(w_g), const(w_glu), const(w_pa), const(w_pb), const(w_out), const(final_g)],
        out_specs=pl.BlockSpec((tm, d), row),
        out_shape=jax.ShapeDtypeStruct((t, d), _F32),
        scratch_shapes=[pltpu.VMEM((d_ssm // V7X_LANES, tm, V7X_LANES), _F32)],
        compiler_params=_cparams("parallel"),
        name="merge",
    )(x2d, oa, yg, g, w_g, w_glu, w_pa, w_pb, w_out, final_g)


def _largest_tile(t, cap):
    tm = min(t, cap)
    while t % tm:
        tm //= 2
    return tm


def kernel(x_prompt, x_sample, cache_k, cache_v, state_ssm_re, state_ssm_im, page_table, norm_g, w_in,
           lam_re, lam_im, log_dt, b_re, b_im, c_re, c_im, d_skip, w_glu, w_pa, w_pb, w_out, final_norm_g):
    b, l, d = x_prompt.shape
    bd, s_len, _ = x_sample.shape
    depth = norm_g.shape[0]
    _, _, page, heads, head_dim = cache_k.shape
    n_groups, p_state = lam_re.shape[1:]
    hg = b_re.shape[-1]
    d_attn, d_ssm = heads * head_dim, n_groups * hg
    n_pages = page_table.shape[1]
    nbp = n_pages * page // MOBA_BLOCK
    assert w_in.shape[-1] == 4 * d_attn + 2 * d_ssm + 2 * d
    assert l % MOBA_BLOCK == 0 and MOBA_BLOCK % page == 0 and (n_pages * page) % MOBA_BLOCK == 0
    assert s_len <= SSM_CHUNK and SSM_CHUNK * hg == V7X_MXU_DIM and l % SSM_CHUNK == 0
    assert V7X_LANES % hg == 0 and d_ssm % V7X_LANES == 0 and d_attn % V7X_MXU_DIM == 0

    tm_p = _largest_tile(l, 1024)
    tm_m = _largest_tile(b * l, 512)
    assert tm_p % MOBA_BLOCK == 0 and tm_m % (SSM_CHUNK * V7X_SUBLANES) == 0
    tm_s = bd * s_len
    xp = x_prompt.reshape(b * l, d)
    xs = x_sample.reshape(bd * s_len, d)
    cache_kt = cache_k.transpose(0, 1, 3, 4, 2)
    cache_vt = cache_v.transpose(0, 1, 3, 4, 2)
    zeros_state = jnp.zeros((n_groups, b, p_state), _F32)
    fg = final_norm_g[None, :]
    proj_kw = dict(d_attn=d_attn, d_ssm=d_ssm, head_dim=head_dim, hg=hg)

    kp, vp, ks, vs, hpr, hpi, hsr, hsi = ([] for _ in range(8))
    for layer in range(depth):
        wl = w_in[layer]
        o_z, o_u, o_zb, o_ga = 3 * d_attn, 4 * d_attn, 4 * d_attn + d_ssm, 4 * d_attn + 2 * d_ssm
        w_a = jnp.concatenate([wl[:, :o_z], wl[:, o_u:o_zb]], axis=1).astype(_BF16)
        w_g = jnp.concatenate([wl[:, o_z:o_u], wl[:, o_zb:o_ga], wl[:, o_ga:]], axis=1).astype(_BF16)
        gl = norm_g[layer][None, :]
        merge_w = (gl, w_g, w_glu[layer].astype(_BF16), w_pa[layer].astype(_BF16),
                   w_pb[layer].astype(_BF16), w_out[layer].astype(_BF16), fg)
        last = layer == depth - 1
        prep = _ssm_prep_call(lam_re[layer], lam_im[layer], log_dt[layer], b_re[layer], b_im[layer],
                              c_re[layer], c_im[layer])

        qb, k, v, kb, vb, xg, kmean = _proj_call(xp, gl, w_a, tok=SSM_CHUNK, tm=tm_p, with_kmean=True, seq_len=l,
                                                 **proj_kw)
        oa = _moba_prompt_call(qb.reshape(b, l, d_attn), kb.reshape(b, l, d_attn), vb.reshape(b, l, d_attn),
                               kmean.reshape(b, l // MOBA_BLOCK, d_attn), head_dim=head_dim)
        yg, h_re, h_im = _ssm_call(xg, _ssm_tables(prep, d_skip[layer], SSM_CHUNK), zeros_state, zeros_state,
                                   nb=b, nc=l // SSM_CHUNK, gb=4)
        xp = _merge_call(xp, oa.reshape(b * l, d_attn), yg, *merge_w, hg=hg, tok=SSM_CHUNK, tm=tm_m,
                         final_norm=last)
        kp.append(k.reshape(b, heads, head_dim, l).transpose(0, 3, 1, 2))
        vp.append(v.reshape(b, heads, head_dim, l).transpose(0, 3, 1, 2))
        hpr.append(h_re.transpose(1, 0, 2))
        hpi.append(h_im.transpose(1, 0, 2))

        qb, k, v, kb, vb, xg = _proj_call(xs, gl, w_a, tok=s_len, tm=tm_s, with_kmean=False, **proj_kw)
        kmean_t = _kmean_pages_call(cache_kt, page_table, layer, pages_per_step=min(32, n_pages))
        sel = _select_call(qb.reshape(bd, s_len, d_attn), kmean_t.reshape(bd, d_attn, V7X_LANES),
                           heads=heads, head_dim=head_dim, nbp=nbp)
        sel = sel[:, :, :MOBA_TOPK].reshape(-1)
        to_heads = lambda a: a.reshape(bd, s_len, heads, head_dim).transpose(0, 2, 1, 3)
        oa = _sample_attend_call(page_table, sel, to_heads(qb), to_heads(kb), to_heads(vb),
                                 cache_kt, cache_vt, layer)
        oa = oa.transpose(0, 2, 1, 3).reshape(bd * s_len, d_attn)
        yg, h_re, h_im = _ssm_call(xg, _ssm_tables(prep, d_skip[layer], s_len),
                                   state_ssm_re[layer].transpose(1, 0, 2), state_ssm_im[layer].transpose(1, 0, 2),
                                   nb=bd, nc=1, gb=4)
        xs = _merge_call(xs, oa, yg, *merge_w, hg=hg, tok=s_len, tm=tm_s, final_norm=last)
        ks.append(k.reshape(bd, s_len, heads, head_dim))
        vs.append(v.reshape(bd, s_len, heads, head_dim))
        hsr.append(h_re.transpose(1, 0, 2))
        hsi.append(h_im.transpose(1, 0, 2))

    return (xp.reshape(b, l, d), xs.reshape(bd, s_len, d),
            jnp.stack(kp), jnp.stack(vp), jnp.stack(ks), jnp.stack(vs),
            jnp.stack(hpr), jnp.stack(hpi), jnp.stack(hsr), jnp.stack(hsi))
```

```python
import functools

import jax
import jax.numpy as jnp
from jax import lax
from jax.experimental import pallas as pl
from jax.experimental.pallas import tpu as pltpu

MOBA_BLOCK = 256
MOBA_TOPK = 3
RMS_EPS = 1e-6

V7X_LANES = 128
V7X_SUBLANES = 8
V7X_MXU_DIM = 256
V7X_VMEM_LIMIT_BYTES = 56 * 1024 * 1024

SSM_CHUNK = 16
MASKED_LOGIT = -1e30
LOG2_E = 1.4426950408889634

_F32 = jnp.float32
_BF16 = jnp.bfloat16


def _cparams(*sem):
    return pltpu.CompilerParams(dimension_semantics=sem, vmem_limit_bytes=V7X_VMEM_LIMIT_BYTES)


def _rmsnorm(x, g):
    return x * lax.rsqrt(jnp.mean(x * x, axis=-1, keepdims=True) + RMS_EPS) * g


def _dot(a, b):
    return jnp.dot(a, b, preferred_element_type=_F32)


def _dot_nt(a, b):
    return lax.dot_general(a, b, (((1,), (1,)), ((), ())), preferred_element_type=_F32)


def _twice(a):
    return jnp.concatenate([a, a], axis=1)


def _tokens_to_chunks(u, slab_ref, xg_ref, *, hg, tok):
    tm, c = u.shape
    n = tm // tok
    gpl = V7X_LANES // hg
    lane_slot = lax.broadcasted_iota(jnp.int32, (n, V7X_LANES), 1) // hg
    for cb in range(c // V7X_LANES):
        slab_ref[cb] = u[:, cb * V7X_LANES:(cb + 1) * V7X_LANES]
    for cb in range(c // V7X_LANES):
        rows = [slab_ref[cb, pl.ds(t, n, stride=tok), :] for t in range(tok)]
        rolled = {}
        for gi in range(gpl):
            cols = []
            for cv in range(SSM_CHUNK // gpl):
                acc = jnp.zeros((n, V7X_LANES), _F32)
                for jj in range(gpl):
                    t = cv * gpl + jj
                    if t >= tok:
                        continue
                    shift = ((jj - gi) * hg) % V7X_LANES
                    if (t, shift) not in rolled:
                        rolled[(t, shift)] = pltpu.roll(rows[t], shift, 1) if shift else rows[t]
                    acc = jnp.where(lane_slot == jj, rolled[(t, shift)], acc)
                cols.append(acc)
            xg_ref[cb * gpl + gi] = jnp.concatenate(cols, axis=1)


def _chunks_to_tokens(y_ref, slab_ref, *, hg, tok):
    g, n, _ = y_ref.shape
    gpl = V7X_LANES // hg
    lane_slot = lax.broadcasted_iota(jnp.int32, (n, V7X_LANES), 1) // hg
    n_slabs = g // gpl
    for cb in range(n_slabs):
        for cv in range(SSM_CHUNK // gpl):
            src = [y_ref[cb * gpl + gi, :, cv * V7X_LANES:(cv + 1) * V7X_LANES] for gi in range(gpl)]
            rolled = {}
            for jj in range(gpl):
                t = cv * gpl + jj
                if t >= tok:
                    continue
                piece = jnp.zeros((n, V7X_LANES), _F32)
                for gi in range(gpl):
                    shift = ((gi - jj) * hg) % V7X_LANES
                    if (gi, shift) not in rolled:
                        rolled[(gi, shift)] = pltpu.roll(src[gi], shift, 1) if shift else src[gi]
                    piece = jnp.where(lane_slot == gi, rolled[(gi, shift)], piece)
                slab_ref[cb, pl.ds(t, n, stride=tok), :] = piece
    return jnp.concatenate([slab_ref[cb] for cb in range(n_slabs)], axis=1)


def _proj_kernel(x_ref, g_ref, w_ref, q_ref, k_ref, v_ref, kb_ref, vb_ref, xg_ref, km_ref, slab_ref,
                 *, d_attn, d_ssm, head_dim, blk, hg, tok, kv_token_minor):
    hb = _rmsnorm(x_ref[...], g_ref[...]).astype(_BF16)
    q = _dot(hb, w_ref[:, 0:d_attn])
    k = _dot(hb, w_ref[:, d_attn:2 * d_attn])
    v = _dot(hb, w_ref[:, 2 * d_attn:3 * d_attn])
    u = _dot(hb, w_ref[:, 3 * d_attn:3 * d_attn + d_ssm])
    q_ref[...] = (q * (head_dim ** -0.5 * LOG2_E)).astype(_BF16)
    k_ref[...] = k.T if kv_token_minor else k
    v_ref[...] = v.T if kv_token_minor else v
    kb_ref[...] = k.astype(_BF16)
    vb_ref[...] = v.astype(_BF16)
    _tokens_to_chunks(u, slab_ref, xg_ref, hg=hg, tok=tok)
    if km_ref is not None:
        tm = k.shape[0]
        km_ref[0] = jnp.sum(k.reshape(tm // blk, blk, d_attn), axis=1) * (1.0 / blk)


def _proj_call(x2d, g, w_a, *, d_attn, d_ssm, head_dim, hg, tok, tm, with_kmean, seq_len=None):
    t, d = x2d.shape
    n_tiles = t // tm
    n_groups = d_ssm // hg
    row = lambda i: (i, 0)
    w = SSM_CHUNK * hg
    if seq_len is None:
        kv_shape = jax.ShapeDtypeStruct((t, d_attn), _F32)
        kv_spec = pl.BlockSpec((tm, d_attn), row)
        xg_shape = jax.ShapeDtypeStruct((n_groups, t // tok, w), _F32)
        xg_spec = pl.BlockSpec((n_groups, tm // tok, w), lambda i: (0, i, 0))
    else:
        tps = seq_len // tm
        kv_shape = jax.ShapeDtypeStruct((t // seq_len, d_attn, seq_len), _F32)
        kv_spec = pl.BlockSpec((None, d_attn, tm), lambda i: (i // tps, 0, i % tps))
        xg_shape = jax.ShapeDtypeStruct((n_groups, t // tok // tps, tps * w), _F32)
        xg_spec = pl.BlockSpec((n_groups, tm // tok, w), lambda i: (0, i // tps, i % tps))
    out_shape = [
        jax.ShapeDtypeStruct((t, d_attn), _BF16),
        kv_shape,
        kv_shape,
        jax.ShapeDtypeStruct((t, d_attn), _BF16),
        jax.ShapeDtypeStruct((t, d_attn), _BF16),
        xg_shape,
    ]
    out_specs = [pl.BlockSpec((tm, d_attn), row), kv_spec, kv_spec,
                 pl.BlockSpec((tm, d_attn), row), pl.BlockSpec((tm, d_attn), row), xg_spec]
    body = functools.partial(_proj_kernel, d_attn=d_attn, d_ssm=d_ssm, head_dim=head_dim, blk=MOBA_BLOCK,
                             hg=hg, tok=tok, kv_token_minor=seq_len is not None)
    if with_kmean:
        out_shape.append(jax.ShapeDtypeStruct((n_tiles, tm // MOBA_BLOCK, d_attn), _F32))
        out_specs.append(pl.BlockSpec((1, tm // MOBA_BLOCK, d_attn), lambda i: (i, 0, 0)))
        kern = body
    else:
        kern = lambda *refs: body(*refs[:-1], None, refs[-1])
    return pl.pallas_call(
        kern,
        grid=(n_tiles,),
        in_specs=[pl.BlockSpec((tm, d), row),
                  pl.BlockSpec((1, d), lambda i: (0, 0)),
                  pl.BlockSpec(w_a.shape, lambda i: (0, 0))],
        out_specs=out_specs,
        out_shape=out_shape,
        scratch_shapes=[pltpu.VMEM((d_ssm // V7X_LANES, tm, V7X_LANES), _F32)],
        compiler_params=_cparams("parallel"),
        name="proj",
    )(x2d, g, w_a)


def _moba_prompt_kernel(q_ref, k_ref, v_ref, km_ref, o_ref, kaug_ref, m_ref, l_ref, acc_ref,
                        *, nb, head_dim, blk):
    i = pl.program_id(2)
    width = q_ref.shape[-1]
    pw = V7X_LANES
    n_pairs = width // pw
    hpp = pw // head_dim
    prow = hpp * blk
    rows = n_pairs * prow
    seq = k_ref.shape[0]

    @pl.when(i == 0)
    def _():
        key_blk = lax.broadcasted_iota(jnp.int32, (seq, pw), 0) // blk
        onehot = jnp.where(key_blk == lax.broadcasted_iota(jnp.int32, (seq, pw), 1), 1.0, 0.0).astype(_BF16)
        for p in range(n_pairs):
            kaug_ref[p, :, 0:pw] = k_ref[:, p * pw:(p + 1) * pw]
            kaug_ref[p, :, pw:2 * pw] = onehot

    q = q_ref[...]
    lane_head = lax.broadcasted_iota(jnp.int32, (blk, pw), 1) // head_dim
    q2 = []
    for p in range(n_pairs):
        qp = q[:, p * pw:(p + 1) * pw]
        q2.append(jnp.concatenate([jnp.where(lane_head == h, qp, jnp.zeros_like(qp)) for h in range(hpp)], axis=0))

    km = km_ref[...].astype(_BF16)
    gate = jnp.concatenate([_dot_nt(km[:, p * pw:(p + 1) * pw], q2[p]) for p in range(n_pairs)], axis=1)
    n_iota = lax.broadcasted_iota(jnp.int32, (nb, rows), 0)
    gate = jnp.where(n_iota < i, gate, -jnp.inf)
    rank = jnp.zeros((nb, rows), jnp.int32)
    for m in range(nb):
        gm = gate[m:m + 1, :]
        tie_first = jnp.where(m < n_iota, 1, 0)
        rank = rank + jnp.where(gm > gate, 1, jnp.where(gm == gate, tie_first, 0))
    keep = jnp.where(n_iota < i, jnp.where(rank < MOBA_TOPK, 1, 0), jnp.where(n_iota == i, 1, 0))
    bias = jnp.where(keep == 1, 0.0, MASKED_LOGIT).astype(_F32)
    bias = jnp.concatenate([bias, jnp.full((pw - nb, rows), MASKED_LOGIT, _F32)], axis=0).T
    qaug = [jnp.concatenate([q2[p], bias[p * prow:(p + 1) * prow].astype(_BF16)], axis=1) for p in range(n_pairs)]

    span = 2 * blk
    n_chunks = span // pw

    def logits(kstart):
        return jnp.concatenate([_dot_nt(qaug[p], kaug_ref[p, pl.ds(kstart, span), :]) for p in range(n_pairs)],
                               axis=0)

    def lane_chunks(a):
        return [a[:, c * pw:(c + 1) * pw] for c in range(n_chunks)]

    def row_max(s):
        return jnp.broadcast_to(jnp.max(functools.reduce(jnp.maximum, lane_chunks(s)), axis=1, keepdims=True),
                                (rows, pw))

    def widen(a, n):
        return jnp.concatenate([a] * n, axis=1)

    own_span = i // 2
    start = pl.multiple_of(own_span * span, span)
    s = logits(start)
    q_off = lax.broadcasted_iota(jnp.int32, (rows, span), 0) % blk + (i - 2 * own_span) * blk
    k_off = lax.broadcasted_iota(jnp.int32, (rows, span), 1)
    s = jnp.where(k_off <= q_off, s, MASKED_LOGIT)
    m0 = row_max(s)
    p0 = jnp.exp2(s - widen(m0, n_chunks))
    m_ref[...] = m0
    l_ref[...] = functools.reduce(jnp.add, lane_chunks(p0))
    acc_ref[...] = _dot(p0.astype(_BF16), v_ref[pl.ds(start, span), :])

    for n in range(nb // 2 - 1):
        @pl.when(n < own_span)
        def _(n=n):
            s = logits(n * span)
            m_old = m_ref[...]
            m_new = jnp.maximum(m_old, row_max(s))
            alpha = jnp.exp2(m_old - m_new)
            p = jnp.exp2(s - widen(m_new, n_chunks))
            l_ref[...] = alpha * l_ref[...] + functools.reduce(jnp.add, lane_chunks(p))
            acc_ref[...] = _twice(alpha) * acc_ref[...] + _dot(p.astype(_BF16), v_ref[n * span:(n + 1) * span, :])
            m_ref[...] = m_new

    o4 = acc_ref[...] / jnp.sum(l_ref[...], axis=1, keepdims=True)
    lane_head4 = lax.broadcasted_iota(jnp.int32, (blk, width), 1) // head_dim
    out = jnp.zeros((blk, width), _F32)
    for h in range(width // head_dim):
        out = out + jnp.where(lane_head4 == h, o4[h * blk:(h + 1) * blk, :], 0.0)
    o_ref[...] = out.astype(o_ref.dtype)


def _moba_prompt_call(qb, kb, vb, kmean, *, head_dim):
    b, l, d_attn = qb.shape
    blk = MOBA_BLOCK
    nb = l // blk
    assert nb % 2 == 0
    width = V7X_MXU_DIM
    rows = (width // head_dim) * blk
    body = functools.partial(_moba_prompt_kernel, nb=nb, head_dim=head_dim, blk=blk)
    return pl.pallas_call(
        body,
        grid=(b, d_attn // width, nb),
        in_specs=[pl.BlockSpec((None, blk, width), lambda bi, c, i: (bi, i, c)),
                  pl.BlockSpec((None, l, width), lambda bi, c, i: (bi, 0, c)),
                  pl.BlockSpec((None, l, width), lambda bi, c, i: (bi, 0, c)),
                  pl.BlockSpec((None, nb, width), lambda bi, c, i: (bi, 0, c))],
        out_specs=pl.BlockSpec((None, blk, width), lambda bi, c, i: (bi, i, c)),
        out_shape=jax.ShapeDtypeStruct((b, l, d_attn), _BF16),
        scratch_shapes=[pltpu.VMEM((width // V7X_LANES, l, 2 * V7X_LANES), _BF16),
                        pltpu.VMEM((rows, V7X_LANES), _F32),
                        pltpu.VMEM((rows, V7X_LANES), _F32),
                        pltpu.VMEM((rows, width), _F32)],
        compiler_params=_cparams("parallel", "parallel", "arbitrary"),
        name="moba_prompt",
    )(qb, kb, vb, kmean)


def _kmean_pages_kernel(pt_ref, ck_ref, out_ref, buf, sem, *, layer, pages_per_step, pages_per_block, steps,
                        inv_blk):
    i = pl.program_id(0)
    n_total = pl.num_programs(0)
    c = i % steps
    bps = pages_per_step // pages_per_block

    def copies(step, slot):
        b = step // steps
        first = (step % steps) * pages_per_step
        return [pltpu.make_async_copy(ck_ref.at[layer, pt_ref[b, first + r]], buf.at[slot, r], sem.at[slot])
                for r in range(pages_per_step)]

    def reduce_pages(slot):
        lane = lax.broadcasted_iota(jnp.int32, out_ref.shape, 2)
        acc = out_ref[...]
        for j in range(bps):
            tot = buf[slot, j * pages_per_block]
            for r in range(1, pages_per_block):
                tot = tot + buf[slot, j * pages_per_block + r]
            mean = jnp.sum(tot, axis=-1, keepdims=True) * inv_blk
            acc = jnp.where(lane == c * bps + j, mean, acc)
        out_ref[...] = acc

    @pl.when(i == 0)
    def _():
        for cp in copies(0, 0):
            cp.start()

    @pl.when(c == 0)
    def _():
        out_ref[...] = jnp.zeros_like(out_ref)

    for parity in range(2):
        @pl.when(i % 2 == parity)
        def _(parity=parity):
            @pl.when(i + 1 < n_total)
            def _():
                for cp in copies(i + 1, 1 - parity):
                    cp.start()
            for cp in copies(i, parity):
                cp.wait()
            reduce_pages(parity)


def _kmean_pages_call(cache_kt, page_table, layer, *, pages_per_step):
    _, _, heads, head_dim, page = cache_kt.shape
    bd, n_pages = page_table.shape
    ppb = MOBA_BLOCK // page
    assert n_pages // ppb <= V7X_LANES and n_pages % pages_per_step == 0 and pages_per_step % ppb == 0
    steps = n_pages // pages_per_step
    grid_spec = pltpu.PrefetchScalarGridSpec(
        num_scalar_prefetch=1,
        grid=(bd * steps,),
        in_specs=[pl.BlockSpec(memory_space=pl.ANY)],
        out_specs=pl.BlockSpec((None, heads, head_dim, V7X_LANES), lambda i, pt: (i // steps, 0, 0, 0)),
        scratch_shapes=[pltpu.VMEM((2, pages_per_step, heads, head_dim, page), _F32),
                        pltpu.SemaphoreType.DMA((2,))],
    )
    body = functools.partial(_kmean_pages_kernel, layer=layer, pages_per_step=pages_per_step,
                             pages_per_block=ppb, steps=steps, inv_blk=1.0 / MOBA_BLOCK)
    return pl.pallas_call(
        body,
        grid_spec=grid_spec,
        out_shape=jax.ShapeDtypeStruct((bd, heads, head_dim, V7X_LANES), _F32),
        compiler_params=_cparams("arbitrary"),
        name="kmean_pages",
    )(page_table, cache_kt)


def _select_kernel(qh_ref, kmt_ref, sel_ref, *, nbp):
    rows = qh_ref.shape[0]
    gate = _dot(qh_ref[...], kmt_ref[...].astype(_BF16))
    n_iota = lax.broadcasted_iota(jnp.int32, (rows, V7X_LANES), 1)
    gate = jnp.where(n_iota < nbp, gate, -jnp.inf)
    rank = jnp.zeros((rows, V7X_LANES), jnp.int32)
    for m in range(nbp):
        gm = gate[:, m:m + 1]
        tie_first = jnp.where(m < n_iota, 1, 0)
        rank = rank + jnp.where(gm > gate, 1, jnp.where(gm == gate, tie_first, 0))
    out = jnp.zeros((rows, V7X_LANES), jnp.int32)
    for j in range(MOBA_TOPK):
        hit = jnp.where(n_iota < nbp, jnp.where(rank == j, n_iota, 0), 0)
        out = jnp.where(n_iota == j, jnp.sum(hit, axis=1, keepdims=True), out)
    sel_ref[...] = out


def _select_call(q_s, kmean_t, *, heads, head_dim, nbp):
    bd, s_len, d_attn = q_s.shape
    rows = heads * s_len
    head_of_lane = jnp.arange(d_attn) // head_dim
    qh = jnp.where(head_of_lane[None, None, None, :] == jnp.arange(heads)[None, :, None, None],
                   q_s[:, None, :, :], jnp.zeros((), q_s.dtype)).reshape(bd, rows, d_attn)
    return pl.pallas_call(
        functools.partial(_select_kernel, nbp=nbp),
        grid=(bd,),
        in_specs=[pl.BlockSpec((None, rows, d_attn), lambda b: (b, 0, 0)),
                  pl.BlockSpec((None, d_attn, V7X_LANES), lambda b: (b, 0, 0))],
        out_specs=pl.BlockSpec((None, rows, V7X_LANES), lambda b: (b, 0, 0)),
        out_shape=jax.ShapeDtypeStruct((bd, rows, V7X_LANES), jnp.int32),
        compiler_params=_cparams("parallel"),
        name="moba_select",
    )(qh, kmean_t)


def _sample_attend_kernel(pt_ref, sel_ref, q_ref, kn_ref, vn_ref, ck_ref, cv_ref, o_ref, kbuf, vbuf, sem,
                          *, layer, heads, s_len, page, ppb):
    b = pl.program_id(0)
    n_steps = pl.num_programs(0)
    n_sel = s_len * MOBA_TOPK
    blk = page * ppb

    def copies(seq, buf):
        out = []
        for h in range(heads):
            for slot in range(n_sel):
                blk_idx = sel_ref[(seq * heads + h) * n_sel + slot]
                for r in range(ppb):
                    pg = pt_ref[seq, blk_idx * ppb + r]
                    dst = pl.ds((slot * ppb + r) * page, page)
                    out.append(pltpu.make_async_copy(ck_ref.at[layer, pg, h], kbuf.at[buf, h, :, dst], sem.at[0, buf]))
                    out.append(pltpu.make_async_copy(cv_ref.at[layer, pg, h], vbuf.at[buf, h, :, dst], sem.at[1, buf]))
        return out

    def attend(buf):
        for h in range(heads):
            q = q_ref[h]
            s_past = _dot(q, kbuf[buf, h].astype(_BF16))
            qi = lax.broadcasted_iota(jnp.int32, s_past.shape, 0)
            slot_q = lax.broadcasted_iota(jnp.int32, s_past.shape, 1) // (blk * MOBA_TOPK)
            s_past = jnp.where(qi == slot_q, s_past, MASKED_LOGIT)
            s_own = _dot_nt(q, kn_ref[h])
            causal = (lax.broadcasted_iota(jnp.int32, s_own.shape, 1)
                      <= lax.broadcasted_iota(jnp.int32, s_own.shape, 0))
            s_own = jnp.where(causal, s_own, MASKED_LOGIT)
            m = jnp.maximum(jnp.max(s_past, axis=1, keepdims=True), jnp.max(s_own, axis=1, keepdims=True))
            p_past = jnp.exp2(s_past - m)
            p_own = jnp.exp2(s_own - m)
            denom = jnp.sum(p_past, axis=1, keepdims=True) + jnp.sum(p_own, axis=1, keepdims=True)
            o = _dot_nt(p_past.astype(_BF16), vbuf[buf, h].astype(_BF16)) + _dot(p_own.astype(_BF16), vn_ref[h])
            o_ref[h] = (o / denom).astype(o_ref.dtype)

    @pl.when(b == 0)
    def _():
        for cp in copies(0, 0):
            cp.start()

    for parity in range(2):
        @pl.when(b % 2 == parity)
        def _(parity=parity):
            @pl.when(b + 1 < n_steps)
            def _():
                for cp in copies(b + 1, 1 - parity):
                    cp.start()
            for cp in copies(b, parity):
                cp.wait()
            attend(parity)


def _sample_attend_call(page_table, sel, q_h, kn_h, vn_h, cache_kt, cache_vt, layer):
    bd, heads, s_len, head_dim = q_h.shape
    page = cache_kt.shape[-1]
    ppb = MOBA_BLOCK // page
    n_cols = s_len * MOBA_TOPK * MOBA_BLOCK
    body = functools.partial(_sample_attend_kernel, layer=layer, heads=heads, s_len=s_len, page=page, ppb=ppb)
    per_seq = pl.BlockSpec((None, heads, s_len, head_dim), lambda b, pt, sl: (b, 0, 0, 0))
    grid_spec = pltpu.PrefetchScalarGridSpec(
        num_scalar_prefetch=2,
        grid=(bd,),
        in_specs=[per_seq, per_seq, per_seq,
                  pl.BlockSpec(memory_space=pl.ANY), pl.BlockSpec(memory_space=pl.ANY)],
        out_specs=per_seq,
        scratch_shapes=[pltpu.VMEM((2, heads, head_dim, n_cols), _F32),
                        pltpu.VMEM((2, heads, head_dim, n_cols), _F32),
                        pltpu.SemaphoreType.DMA((2, 2))],
    )
    return pl.pallas_call(
        body,
        grid_spec=grid_spec,
        out_shape=jax.ShapeDtypeStruct((bd, heads, s_len, head_dim), _BF16),
        compiler_params=_cparams("arbitrary"),
        name="moba_sample",
    )(page_table, sel, q_h, kn_h, vn_h, cache_kt, cache_vt)


def _ssm_prep_kernel(lr_ref, li_ref, ldt_ref, brt_ref, bit_ref, cr_ref, ci_ref,
                     m_ref, ber_ref, bei_ref, cpr_ref, cpi_ref, aer_ref, aei_ref, *, tok_variants):
    tc = SSM_CHUNK
    hg = cr_ref.shape[0]
    dt = jnp.exp(ldt_ref[...])
    lr, li = lr_ref[...], li_ref[...]
    mag = jnp.exp(lr * dt)
    a_re, a_im = mag * jnp.cos(li * dt), mag * jnp.sin(li * dt)
    den = lr * lr + li * li
    f_re = ((a_re - 1.0) * lr + a_im * li) / den
    f_im = (a_im * lr - (a_re - 1.0) * li) / den
    brt, bit = brt_ref[...], bit_ref[...]
    bb_re = f_re * brt - f_im * bit
    bb_im = f_re * bit + f_im * brt
    cr, ci = cr_ref[...], ci_ref[...]

    pr, pi = jnp.ones_like(a_re), jnp.zeros_like(a_re)
    ca_re, ca_im, bp_re, bp_im, ap_re, ap_im = [], [], [], [], [], []
    for _ in range(tc + 1):
        ca_re.append(cr * pr - ci * pi)
        ca_im.append(cr * pi + ci * pr)
        bp_re.append(pr * bb_re - pi * bb_im)
        bp_im.append(pr * bb_im + pi * bb_re)
        ap_re.append(pr)
        ap_im.append(pi)
        pr, pi = pr * a_re - pi * a_im, pr * a_im + pi * a_re

    hi = lax.Precision.HIGHEST
    nt = (((1,), (1,)), ((), ()))
    lag_rows = lambda parts: jnp.concatenate(parts[:tc], axis=0)
    base = (lax.dot_general(bb_re, lag_rows(ca_re), nt, precision=hi, preferred_element_type=_F32)
            - lax.dot_general(bb_im, lag_rows(ca_im), nt, precision=hi, preferred_element_type=_F32))
    lane = lax.broadcasted_iota(jnp.int32, base.shape, 1)
    for s in range(tc):
        shifted = pltpu.roll(base, s * hg, 1) if s else base
        m_ref[s * hg:(s + 1) * hg, :] = jnp.where(lane >= s * hg, shifted, 0.0).astype(m_ref.dtype)

    zero = jnp.zeros_like(bb_re)
    for v, n in enumerate(tok_variants):
        ber_ref[v] = jnp.concatenate([bp_re[n - 1 - s] if s < n else zero for s in range(tc)],
                                     axis=0).astype(ber_ref.dtype)
        bei_ref[v] = jnp.concatenate([bp_im[n - 1 - s] if s < n else zero for s in range(tc)],
                                     axis=0).astype(bei_ref.dtype)
        aer_ref[v] = ap_re[n]
        aei_ref[v] = ap_im[n]
    cpr_ref[...] = jnp.concatenate(ca_re[1:], axis=0).astype(cpr_ref.dtype)
    cpi_ref[...] = (-jnp.concatenate(ca_im[1:], axis=0)).astype(cpi_ref.dtype)


def _ssm_prep_call(lam_re, lam_im, log_dt, b_re, b_im, c_re, c_im, *, tok_variants):
    g, p = lam_re.shape
    hg = b_re.shape[-1]
    w = SSM_CHUNK * hg
    nv = len(tok_variants)
    body = functools.partial(_ssm_prep_kernel, tok_variants=tok_variants)

    def full(*dims):
        return pl.BlockSpec((None,) + dims, lambda i: (i,) + (0,) * len(dims))

    return pl.pallas_call(
        body,
        grid=(g,),
        in_specs=[full(1, p), full(1, p), full(1, 1), full(hg, p), full(hg, p), full(hg, p), full(hg, p)],
        out_specs=[full(w, w), full(nv, w, p), full(nv, w, p), full(w, p), full(w, p), full(nv, 1, p), full(nv, 1, p)],
        out_shape=[jax.ShapeDtypeStruct((g, w, w), _BF16),
                   jax.ShapeDtypeStruct((g, nv, w, p), _BF16),
                   jax.ShapeDtypeStruct((g, nv, w, p), _BF16),
                   jax.ShapeDtypeStruct((g, w, p), _BF16),
                   jax.ShapeDtypeStruct((g, w, p), _BF16),
                   jax.ShapeDtypeStruct((g, nv, 1, p), _F32),
                   jax.ShapeDtypeStruct((g, nv, 1, p), _F32)],
        compiler_params=_cparams("parallel"),
        name="ssm_prep",
    )(lam_re[:, None, :], lam_im[:, None, :], log_dt[:, None, None],
      b_re.transpose(0, 2, 1), b_im.transpose(0, 2, 1), c_re, c_im)


def _ssm_tables(prep, d_skip, variant):
    m_intra, bend_re, bend_im, cp_re, cp_im, aend_re, aend_im = prep
    dvec = jnp.tile(d_skip[:, None, :], (1, SSM_CHUNK, 1)).reshape(d_skip.shape[0], 1, -1)
    return (m_intra, bend_re[:, variant], bend_im[:, variant], cp_re, cp_im,
            aend_re[:, variant], aend_im[:, variant], dvec)


def _pow_static(z_re, z_im, n):
    out = None
    while n:
        if n & 1:
            out = (z_re, z_im) if out is None else (out[0] * z_re - out[1] * z_im, out[0] * z_im + out[1] * z_re)
        n >>= 1
        if n:
            z_re, z_im = z_re * z_re - z_im * z_im, 2.0 * z_re * z_im
    return out


def _ssm_kernel(x_ref, mt_ref, ber_ref, bei_ref, cpr_ref, cpi_ref, aer_ref, aei_ref, dv_ref,
                h0r_ref, h0i_ref, y_ref, hr_ref, hi_ref, sr, si, hsr, hsi, *, gb, nb, nc, nseg):
    for g in range(gb):
        xb = x_ref[g].astype(_BF16)
        sr[g] = _dot(xb, ber_ref[g])
        si[g] = _dot(xb, bei_ref[g])

    a_re, a_im = aer_ref[...], aei_ref[...]
    cmul = lambda xr, xi, yr, yi: (xr * yr - xi * yi, xr * yi + xi * yr)
    if nc == 1:
        h_re, h_im = h0r_ref[...], h0i_ref[...]
        hsr[...] = h_re
        hsi[...] = h_im
        d_re, d_im = cmul(a_re, a_im, h_re, h_im)
        hr_ref[...] = d_re + sr[...]
        hi_ref[...] = d_im + si[...]
    else:
        cps = nc // nseg
        tile = lambda b, jj: pl.ds(pl.multiple_of((b * cps + jj) * nseg, nseg), nseg)

        def local_step(jj, carry):
            nxt = []
            for b in range(nb):
                h_re, h_im = carry[b]
                rows = tile(b, jj)
                hsr[:, rows, :] = h_re
                hsi[:, rows, :] = h_im
                d_re, d_im = cmul(a_re, a_im, h_re, h_im)
                nxt.append((d_re + sr[:, rows, :], d_im + si[:, rows, :]))
            return tuple(nxt)

        zero = jnp.zeros((gb, nseg, a_re.shape[-1]), _F32)
        seg_end = lax.fori_loop(0, cps, local_step, tuple((zero, zero) for _ in range(nb)))

        s_re, s_im = _pow_static(a_re, a_im, cps)
        seg_id = lax.broadcasted_iota(jnp.int32, zero.shape, 1)
        starts = []
        for b in range(nb):
            c_re, c_im = h0r_ref[:, b:b + 1, :], h0i_ref[:, b:b + 1, :]
            st_re, st_im = zero, zero
            for sgm in range(nseg):
                st_re = jnp.where(seg_id == sgm, c_re, st_re)
                st_im = jnp.where(seg_id == sgm, c_im, st_im)
                d_re, d_im = cmul(s_re, s_im, c_re, c_im)
                c_re = d_re + seg_end[b][0][:, sgm:sgm + 1, :]
                c_im = d_im + seg_end[b][1][:, sgm:sgm + 1, :]
            starts.append((st_re, st_im))
            hr_ref[:, b:b + 1, :] = c_re
            hi_ref[:, b:b + 1, :] = c_im

        def fix_step(jj, carry):
            p_re, p_im = carry
            for b in range(nb):
                rows = tile(b, jj)
                d_re, d_im = cmul(p_re, p_im, starts[b][0], starts[b][1])
                hsr[:, rows, :] = hsr[:, rows, :] + d_re
                hsi[:, rows, :] = hsi[:, rows, :] + d_im
            return cmul(p_re, p_im, a_re, a_im)

        lax.fori_loop(0, cps, fix_step, (jnp.ones_like(a_re), jnp.zeros_like(a_re)))

    for g in range(gb):
        x = x_ref[g]
        y = _dot(x.astype(_BF16), mt_ref[g])
        y = y + _dot_nt(hsr[g].astype(_BF16), cpr_ref[g]) + _dot_nt(hsi[g].astype(_BF16), cpi_ref[g])
        y_ref[g] = y + x * dv_ref[g]


def _ssm_call(xg, tables, h0_re, h0_im, *, nb, nc, nseg, gb):
    m_intra, bend_re, bend_im, cp_re, cp_im, aend_re, aend_im, dvec = tables
    g, n, w = xg.shape
    p = bend_re.shape[-1]
    assert n == nb * nc and nc % nseg == 0
    body = functools.partial(_ssm_kernel, gb=gb, nb=nb, nc=nc, nseg=nseg)

    def spec(*dims):
        return pl.BlockSpec((gb,) + dims, lambda i: (i,) + (0,) * len(dims))

    return pl.pallas_call(
        body,
        grid=(g // gb,),
        in_specs=[spec(n, w), spec(w, w), spec(w, p), spec(w, p), spec(w, p), spec(w, p),
                  spec(1, p), spec(1, p), spec(1, w), spec(nb, p), spec(nb, p)],
        out_specs=[spec(n, w), spec(nb, p), spec(nb, p)],
        out_shape=[jax.ShapeDtypeStruct((g, n, w), _F32),
                   jax.ShapeDtypeStruct((g, nb, p), _F32),
                   jax.ShapeDtypeStruct((g, nb, p), _F32)],
        scratch_shapes=[pltpu.VMEM((gb, n, p), _F32)] * 4,
        compiler_params=_cparams("parallel"),
        name="ssm",
    )(xg, m_intra, bend_re, bend_im, cp_re, cp_im, aend_re, aend_im, dvec, h0_re, h0_im)


def _merge_kernel(x_ref, oa_ref, y_ref, g_ref, wg_ref, wglu_ref, wpa_ref, wpb_ref, wout_ref, fg_ref,
                  o_ref, slab_ref, *, d_attn, d_ssm, hg, tok, final_norm):
    x = x_ref[...]
    d = x.shape[-1]
    hb = _rmsnorm(x, g_ref[...]).astype(_BF16)
    z_a = _dot(hb, wg_ref[:, 0:d_attn])
    ya = oa_ref[...].astype(_F32) * jax.nn.silu(z_a)
    br_a = _dot(ya.astype(_BF16), wpa_ref[...])
    ys = _chunks_to_tokens(y_ref, slab_ref, hg=hg, tok=tok)
    glu = _dot(jax.nn.gelu(ys).astype(_BF16), wglu_ref[...])
    z_b = _dot(hb, wg_ref[:, d_attn:d_attn + d_ssm])
    yb = glu[:, :d_ssm] * jax.nn.sigmoid(glu[:, d_ssm:]) * jax.nn.silu(z_b)
    br_b = _dot(yb.astype(_BF16), wpb_ref[...])
    g_a = _dot(hb, wg_ref[:, d_attn + d_ssm:d_attn + d_ssm + d])
    g_b = _dot(hb, wg_ref[:, d_attn + d_ssm + d:d_attn + d_ssm + 2 * d])
    merged = jax.nn.sigmoid(g_a) * br_a + jax.nn.sigmoid(g_b) * br_b
    out = x + _dot(merged.astype(_BF16), wout_ref[...])
    if final_norm:
        out = _rmsnorm(out, fg_ref[...])
    o_ref[...] = out


def _merge_call(x2d, oa, yg, g, w_g, w_glu, w_pa, w_pb, w_out, final_g, *, hg, tok, tm, final_norm, nseg=1):
    t, d = x2d.shape
    d_attn = oa.shape[1]
    n_groups, n_rows, w = yg.shape
    d_ssm = n_groups * hg
    yg = yg.reshape(n_groups, n_rows // nseg, nseg * w)
    row = lambda i: (i, 0)
    const = lambda a: pl.BlockSpec(a.shape, lambda i: (0, 0), pipeline_mode=pl.Buffered(1))
    body = functools.partial(_merge_kernel, d_attn=d_attn, d_ssm=d_ssm, hg=hg, tok=tok, final_norm=final_norm)
    return pl.pallas_call(
        body,
        grid=(t // tm,),
        in_specs=[pl.BlockSpec((tm, d), row), pl.BlockSpec((tm, d_attn), row),
                  pl.BlockSpec((n_groups, tm // tok, w), lambda i: (0, i // nseg, i % nseg)),
                  const(g), const(w_g), const(w_glu), const(w_pa), const(w_pb), const(w_out), const(final_g)],
        out_specs=pl.BlockSpec((tm, d), row),
        out_shape=jax.ShapeDtypeStruct((t, d), _F32),
        scratch_shapes=[pltpu.VMEM((d_ssm // V7X_LANES, tm, V7X_LANES), _F32)],
        compiler_params=_cparams("parallel"),
        name="merge",
    )(x2d, oa, yg, g, w_g, w_glu, w_pa, w_pb, w_out, final_g)


def kernel(x_prompt, x_sample, cache_k, cache_v, state_ssm_re, state_ssm_im, page_table, norm_g, w_in,
           lam_re, lam_im, log_dt, b_re, b_im, c_re, c_im, d_skip, w_glu, w_pa, w_pb, w_out, final_norm_g):
    b, l, d = x_prompt.shape
    bd, s_len, _ = x_sample.shape
    depth = norm_g.shape[0]
    _, _, page, heads, head_dim = cache_k.shape
    n_groups, p_state = lam_re.shape[1:]
    hg = b_re.shape[-1]
    d_attn, d_ssm = heads * head_dim, n_groups * hg
    n_pages = page_table.shape[1]
    nbp = n_pages * page // MOBA_BLOCK
    assert w_in.shape[-1] == 4 * d_attn + 2 * d_ssm + 2 * d
    assert l % MOBA_BLOCK == 0 and MOBA_BLOCK % page == 0 and (n_pages * page) % MOBA_BLOCK == 0
    assert s_len <= SSM_CHUNK and SSM_CHUNK * hg == V7X_MXU_DIM and l % SSM_CHUNK == 0
    assert V7X_LANES % hg == 0 and d_ssm % V7X_LANES == 0 and d_attn % V7X_MXU_DIM == 0

    nseg = V7X_SUBLANES
    tm_p = l // nseg
    assert l % nseg == 0 and tm_p % MOBA_BLOCK == 0 and tm_p % (SSM_CHUNK * V7X_SUBLANES) == 0
    tm_s = bd * s_len
    xp = x_prompt.reshape(b * l, d)
    xs = x_sample.reshape(bd * s_len, d)
    cache_kt = cache_k.transpose(0, 1, 3, 4, 2)
    cache_vt = cache_v.transpose(0, 1, 3, 4, 2)
    zeros_state = jnp.zeros((n_groups, b, p_state), _F32)
    fg = final_norm_g[None, :]
    proj_kw = dict(d_attn=d_attn, d_ssm=d_ssm, head_dim=head_dim, hg=hg)

    kp, vp, ks, vs, hpr, hpi, hsr, hsi = ([] for _ in range(8))
    for layer in range(depth):
        wl = w_in[layer]
        o_z, o_u, o_zb, o_ga = 3 * d_attn, 4 * d_attn, 4 * d_attn + d_ssm, 4 * d_attn + 2 * d_ssm
        w_a = jnp.concatenate([wl[:, :o_z], wl[:, o_u:o_zb]], axis=1).astype(_BF16)
        w_g = jnp.concatenate([wl[:, o_z:o_u], wl[:, o_zb:o_ga], wl[:, o_ga:]], axis=1).astype(_BF16)
        gl = norm_g[layer][None, :]
        merge_w = (gl, w_g, w_glu[layer].astype(_BF16), w_pa[layer].astype(_BF16),
                   w_pb[layer].astype(_BF16), w_out[layer].astype(_BF16), fg)
        last = layer == depth - 1
        prep = _ssm_prep_call(lam_re[layer], lam_im[layer], log_dt[layer], b_re[layer], b_im[layer],
                              c_re[layer], c_im[layer], tok_variants=(SSM_CHUNK, s_len))

        qb, k, v, kb, vb, xg, kmean = _proj_call(xp, gl, w_a, tok=SSM_CHUNK, tm=tm_p, with_kmean=True, seq_len=l,
                                                 **proj_kw)
        oa = _moba_prompt_call(qb.reshape(b, l, d_attn), kb.reshape(b, l, d_attn), vb.reshape(b, l, d_attn),
                               kmean.reshape(b, l // MOBA_BLOCK, d_attn), head_dim=head_dim)
        yg, h_re, h_im = _ssm_call(xg.reshape(n_groups, -1, SSM_CHUNK * hg), _ssm_tables(prep, d_skip[layer], 0),
                                   zeros_state, zeros_state, nb=b, nc=l // SSM_CHUNK, nseg=nseg, gb=4)
        xp = _merge_call(xp, oa.reshape(b * l, d_attn), yg, *merge_w, hg=hg, tok=SSM_CHUNK, tm=tm_p,
                         final_norm=last, nseg=nseg)
        kp.append(k.reshape(b, heads, head_dim, l).transpose(0, 3, 1, 2))
        vp.append(v.reshape(b, heads, head_dim, l).transpose(0, 3, 1, 2))
        hpr.append(h_re.transpose(1, 0, 2))
        hpi.append(h_im.transpose(1, 0, 2))

        qb, k, v, kb, vb, xg = _proj_call(xs, gl, w_a, tok=s_len, tm=tm_s, with_kmean=False, **proj_kw)
        kmean_t = _kmean_pages_call(cache_kt, page_table, layer, pages_per_step=min(32, n_pages))
        sel = _select_call(qb.reshape(bd, s_len, d_attn), kmean_t.reshape(bd, d_attn, V7X_LANES),
                           heads=heads, head_dim=head_dim, nbp=nbp)
        sel = sel[:, :, :MOBA_TOPK].reshape(-1)
        to_heads = lambda a: a.reshape(bd, s_len, heads, head_dim).transpose(0, 2, 1, 3)
        oa = _sample_attend_call(page_table, sel, to_heads(qb), to_heads(kb), to_heads(vb),
                                 cache_kt, cache_vt, layer)
        oa = oa.transpose(0, 2, 1, 3).reshape(bd * s_len, d_attn)
        yg, h_re, h_im = _ssm_call(xg, _ssm_tables(prep, d_skip[layer], 1),
                                   state_ssm_re[layer].transpose(1, 0, 2), state_ssm_im[layer].transpose(1, 0, 2),
                                   nb=bd, nc=1, nseg=1, gb=4)
        xs = _merge_call(xs, oa, yg, *merge_w, hg=hg, tok=s_len, tm=tm_s, final_norm=last)
        ks.append(k.reshape(bd, s_len, heads, head_dim))
        vs.append(v.reshape(bd, s_len, heads, head_dim))
        hsr.append(h_re.transpose(1, 0, 2))
        hsi.append(h_im.transpose(1, 0, 2))

    return (xp.reshape(b, l, d), xs.reshape(bd, s_len, d),
            jnp.stack(kp), jnp.stack(vp), jnp.stack(ks), jnp.stack(vs),
            jnp.stack(hpr), jnp.stack(hpi), jnp.stack(hsr), jnp.stack(hsi))
```

```python
import functools

import jax
import jax.numpy as jnp
from jax import lax
from jax.experimental import pallas as pl
from jax.experimental.pallas import tpu as pltpu

MOBA_BLOCK = 256
MOBA_TOPK = 3
RMS_EPS = 1e-6

V7X_LANES = 128
V7X_SUBLANES = 8
V7X_MXU_DIM = 256
V7X_VMEM_LIMIT_BYTES = 56 * 1024 * 1024

SSM_CHUNK = 16
MASKED_LOGIT = -1e30
LOG2_E = 1.4426950408889634

_F32 = jnp.float32
_BF16 = jnp.bfloat16


def _cparams(*sem):
    return pltpu.CompilerParams(dimension_semantics=sem, vmem_limit_bytes=V7X_VMEM_LIMIT_BYTES)


def _rmsnorm(x, g):
    return x * lax.rsqrt(jnp.mean(x * x, axis=-1, keepdims=True) + RMS_EPS) * g


def _dot(a, b):
    return jnp.dot(a, b, preferred_element_type=_F32)


def _dot_nt(a, b):
    return lax.dot_general(a, b, (((1,), (1,)), ((), ())), preferred_element_type=_F32)


def _twice(a):
    return jnp.concatenate([a, a], axis=1)


def _tokens_to_chunks(u, slab_ref, xg_ref, *, hg, tok):
    tm, c = u.shape
    n = tm // tok
    gpl = V7X_LANES // hg
    lane_slot = lax.broadcasted_iota(jnp.int32, (n, V7X_LANES), 1) // hg
    for cb in range(c // V7X_LANES):
        slab_ref[cb] = u[:, cb * V7X_LANES:(cb + 1) * V7X_LANES]
    for cb in range(c // V7X_LANES):
        rows = [slab_ref[cb, pl.ds(t, n, stride=tok), :] for t in range(tok)]
        rolled = {}
        for gi in range(gpl):
            cols = []
            for cv in range(SSM_CHUNK // gpl):
                acc = jnp.zeros((n, V7X_LANES), _F32)
                for jj in range(gpl):
                    t = cv * gpl + jj
                    if t >= tok:
                        continue
                    shift = ((jj - gi) * hg) % V7X_LANES
                    if (t, shift) not in rolled:
                        rolled[(t, shift)] = pltpu.roll(rows[t], shift, 1) if shift else rows[t]
                    acc = jnp.where(lane_slot == jj, rolled[(t, shift)], acc)
                cols.append(acc)
            xg_ref[cb * gpl + gi] = jnp.concatenate(cols, axis=1)


def _chunks_to_tokens(y_ref, slab_ref, *, hg, tok):
    g, n, _ = y_ref.shape
    gpl = V7X_LANES // hg
    lane_slot = lax.broadcasted_iota(jnp.int32, (n, V7X_LANES), 1) // hg
    n_slabs = g // gpl
    for cb in range(n_slabs):
        for cv in range(SSM_CHUNK // gpl):
            src = [y_ref[cb * gpl + gi, :, cv * V7X_LANES:(cv + 1) * V7X_LANES] for gi in range(gpl)]
            rolled = {}
            for jj in range(gpl):
                t = cv * gpl + jj
                if t >= tok:
                    continue
                piece = jnp.zeros((n, V7X_LANES), _F32)
                for gi in range(gpl):
                    shift = ((gi - jj) * hg) % V7X_LANES
                    if (gi, shift) not in rolled:
                        rolled[(gi, shift)] = pltpu.roll(src[gi], shift, 1) if shift else src[gi]
                    piece = jnp.where(lane_slot == gi, rolled[(gi, shift)], piece)
                slab_ref[cb, pl.ds(t, n, stride=tok), :] = piece
    return jnp.concatenate([slab_ref[cb] for cb in range(n_slabs)], axis=1)


def _proj_kernel(x_ref, g_ref, w_ref, q_ref, k_ref, v_ref, kb_ref, vb_ref, xg_ref, km_ref, slab_ref,
                 *, d_attn, d_ssm, head_dim, blk, hg, tok, kv_token_minor):
    hb = _rmsnorm(x_ref[...], g_ref[...]).astype(_BF16)
    q = _dot(hb, w_ref[:, 0:d_attn])
    k = _dot(hb, w_ref[:, d_attn:2 * d_attn])
    v = _dot(hb, w_ref[:, 2 * d_attn:3 * d_attn])
    u = _dot(hb, w_ref[:, 3 * d_attn:3 * d_attn + d_ssm])
    q_ref[...] = (q * (head_dim ** -0.5 * LOG2_E)).astype(_BF16)
    k_ref[...] = k.T if kv_token_minor else k
    v_ref[...] = v.T if kv_token_minor else v
    kb_ref[...] = k.astype(_BF16)
    vb_ref[...] = v.astype(_BF16)
    _tokens_to_chunks(u, slab_ref, xg_ref, hg=hg, tok=tok)
    if km_ref is not None:
        tm = k.shape[0]
        km_ref[0] = jnp.sum(k.reshape(tm // blk, blk, d_attn), axis=1) * (1.0 / blk)


def _proj_call(x2d, g, w_a, *, d_attn, d_ssm, head_dim, hg, tok, tm, with_kmean, seq_len=None):
    t, d = x2d.shape
    n_tiles = t // tm
    n_groups = d_ssm // hg
    row = lambda i: (i, 0)
    w = SSM_CHUNK * hg
    if seq_len is None:
        kv_shape = jax.ShapeDtypeStruct((t, d_attn), _F32)
        kv_spec = pl.BlockSpec((tm, d_attn), row)
        xg_shape = jax.ShapeDtypeStruct((n_groups, t // tok, w), _F32)
        xg_spec = pl.BlockSpec((n_groups, tm // tok, w), lambda i: (0, i, 0))
    else:
        tps = seq_len // tm
        kv_shape = jax.ShapeDtypeStruct((t // seq_len, d_attn, seq_len), _F32)
        kv_spec = pl.BlockSpec((None, d_attn, tm), lambda i: (i // tps, 0, i % tps))
        xg_shape = jax.ShapeDtypeStruct((n_groups, t // tok // tps, tps * w), _F32)
        xg_spec = pl.BlockSpec((n_groups, tm // tok, w), lambda i: (0, i // tps, i % tps))
    out_shape = [
        jax.ShapeDtypeStruct((t, d_attn), _BF16),
        kv_shape,
        kv_shape,
        jax.ShapeDtypeStruct((t, d_attn), _BF16),
        jax.ShapeDtypeStruct((t, d_attn), _BF16),
        xg_shape,
    ]
    out_specs = [pl.BlockSpec((tm, d_attn), row), kv_spec, kv_spec,
                 pl.BlockSpec((tm, d_attn), row), pl.BlockSpec((tm, d_attn), row), xg_spec]
    body = functools.partial(_proj_kernel, d_attn=d_attn, d_ssm=d_ssm, head_dim=head_dim, blk=MOBA_BLOCK,
                             hg=hg, tok=tok, kv_token_minor=seq_len is not None)
    if with_kmean:
        out_shape.append(jax.ShapeDtypeStruct((n_tiles, tm // MOBA_BLOCK, d_attn), _F32))
        out_specs.append(pl.BlockSpec((1, tm // MOBA_BLOCK, d_attn), lambda i: (i, 0, 0)))
        kern = body
    else:
        kern = lambda *refs: body(*refs[:-1], None, refs[-1])
    return pl.pallas_call(
        kern,
        grid=(n_tiles,),
        in_specs=[pl.BlockSpec((tm, d), row),
                  pl.BlockSpec((1, d), lambda i: (0, 0)),
                  pl.BlockSpec(w_a.shape, lambda i: (0, 0))],
        out_specs=out_specs,
        out_shape=out_shape,
        scratch_shapes=[pltpu.VMEM((d_ssm // V7X_LANES, tm, V7X_LANES), _F32)],
        compiler_params=_cparams("parallel"),
        name="proj",
    )(x2d, g, w_a)


def _kmean_pages_step(pt_ref, ck_ref, out_ref, buf, sem, step, n_total, *, layer, pages_per_step, pages_per_block,
                      steps, inv_blk):
    c = step % steps
    bps = pages_per_step // pages_per_block

    def copies(st, slot):
        b = st // steps
        first = (st % steps) * pages_per_step
        return [pltpu.make_async_copy(ck_ref.at[layer, pt_ref[b, first + r]], buf.at[slot, r], sem.at[slot])
                for r in range(pages_per_step)]

    def reduce_pages(slot):
        lane = lax.broadcasted_iota(jnp.int32, out_ref.shape, 2)
        acc = out_ref[...]
        for j in range(bps):
            tot = buf[slot, j * pages_per_block]
            for r in range(1, pages_per_block):
                tot = tot + buf[slot, j * pages_per_block + r]
            mean = jnp.sum(tot, axis=-1, keepdims=True) * inv_blk
            acc = jnp.where(lane == c * bps + j, mean, acc)
        out_ref[...] = acc

    @pl.when(step == 0)
    def _():
        for cp in copies(0, 0):
            cp.start()

    @pl.when(c == 0)
    def _():
        out_ref[...] = jnp.zeros_like(out_ref)

    slot = step % 2

    @pl.when(step + 1 < n_total)
    def _():
        for cp in copies(step + 1, 1 - slot):
            cp.start()

    for cp in copies(step, slot):
        cp.wait()
    reduce_pages(slot)


def _moba_prompt_kernel(pt_ref, q_ref, k_ref, v_ref, km_ref, ck_ref, o_ref, kmt_ref, kaug_ref, m_ref, l_ref, acc_ref,
                        page_buf, page_sem, *, nb, head_dim, blk, pages):
    i = pl.program_id(2)
    step = (pl.program_id(0) * pl.num_programs(1) + pl.program_id(1)) * nb + i
    _kmean_pages_step(pt_ref, ck_ref, kmt_ref, page_buf, page_sem, step,
                      pl.num_programs(0) * pl.num_programs(1) * nb, **pages)
    width = q_ref.shape[-1]
    pw = V7X_LANES
    n_pairs = width // pw
    hpp = pw // head_dim
    prow = hpp * blk
    rows = n_pairs * prow
    seq = k_ref.shape[0]

    @pl.when(i == 0)
    def _():
        key_blk = lax.broadcasted_iota(jnp.int32, (seq, pw), 0) // blk
        onehot = jnp.where(key_blk == lax.broadcasted_iota(jnp.int32, (seq, pw), 1), 1.0, 0.0).astype(_BF16)
        for p in range(n_pairs):
            kaug_ref[p, :, 0:pw] = k_ref[:, p * pw:(p + 1) * pw]
            kaug_ref[p, :, pw:2 * pw] = onehot

    q = q_ref[...]
    lane_head = lax.broadcasted_iota(jnp.int32, (blk, pw), 1) // head_dim
    q2 = []
    for p in range(n_pairs):
        qp = q[:, p * pw:(p + 1) * pw]
        q2.append(jnp.concatenate([jnp.where(lane_head == h, qp, jnp.zeros_like(qp)) for h in range(hpp)], axis=0))

    km = km_ref[...].astype(_BF16)
    gate = jnp.concatenate([_dot_nt(km[:, p * pw:(p + 1) * pw], q2[p]) for p in range(n_pairs)], axis=1)
    n_iota = lax.broadcasted_iota(jnp.int32, (nb, rows), 0)
    gate = jnp.where(n_iota < i, gate, -jnp.inf)
    rank = jnp.zeros((nb, rows), jnp.int32)
    for m in range(nb):
        gm = gate[m:m + 1, :]
        tie_first = jnp.where(m < n_iota, 1, 0)
        rank = rank + jnp.where(gm > gate, 1, jnp.where(gm == gate, tie_first, 0))
    keep = jnp.where(n_iota < i, jnp.where(rank < MOBA_TOPK, 1, 0), jnp.where(n_iota == i, 1, 0))
    bias = jnp.where(keep == 1, 0.0, MASKED_LOGIT).astype(_F32)
    bias = jnp.concatenate([bias, jnp.full((pw - nb, rows), MASKED_LOGIT, _F32)], axis=0).T
    qaug = [jnp.concatenate([q2[p], bias[p * prow:(p + 1) * prow].astype(_BF16)], axis=1) for p in range(n_pairs)]

    span = 2 * blk
    n_chunks = span // pw

    def logits(kstart):
        return jnp.concatenate([_dot_nt(qaug[p], kaug_ref[p, pl.ds(kstart, span), :]) for p in range(n_pairs)],
                               axis=0)

    def lane_chunks(a):
        return [a[:, c * pw:(c + 1) * pw] for c in range(n_chunks)]

    def row_max(s):
        return jnp.broadcast_to(jnp.max(functools.reduce(jnp.maximum, lane_chunks(s)), axis=1, keepdims=True),
                                (rows, pw))

    def widen(a, n):
        return jnp.concatenate([a] * n, axis=1)

    own_span = i // 2
    start = pl.multiple_of(own_span * span, span)
    s = logits(start)
    q_off = lax.broadcasted_iota(jnp.int32, (rows, span), 0) % blk + (i - 2 * own_span) * blk
    k_off = lax.broadcasted_iota(jnp.int32, (rows, span), 1)
    s = jnp.where(k_off <= q_off, s, MASKED_LOGIT)
    m0 = row_max(s)
    p0 = jnp.exp2(s - widen(m0, n_chunks))
    m_ref[...] = m0
    l_ref[...] = functools.reduce(jnp.add, lane_chunks(p0))
    acc_ref[...] = _dot(p0.astype(_BF16), v_ref[pl.ds(start, span), :])

    for n in range(nb // 2 - 1):
        @pl.when(n < own_span)
        def _(n=n):
            s = logits(n * span)
            m_old = m_ref[...]
            m_new = jnp.maximum(m_old, row_max(s))
            alpha = jnp.exp2(m_old - m_new)
            p = jnp.exp2(s - widen(m_new, n_chunks))
            l_ref[...] = alpha * l_ref[...] + functools.reduce(jnp.add, lane_chunks(p))
            acc_ref[...] = _twice(alpha) * acc_ref[...] + _dot(p.astype(_BF16), v_ref[n * span:(n + 1) * span, :])
            m_ref[...] = m_new

    o4 = acc_ref[...] / jnp.sum(l_ref[...], axis=1, keepdims=True)
    lane_head4 = lax.broadcasted_iota(jnp.int32, (blk, width), 1) // head_dim
    out = jnp.zeros((blk, width), _F32)
    for h in range(width // head_dim):
        out = out + jnp.where(lane_head4 == h, o4[h * blk:(h + 1) * blk, :], 0.0)
    o_ref[...] = out.astype(o_ref.dtype)


def _moba_prompt_call(qb, kb, vb, kmean, cache_kt, page_table, layer, *, head_dim):
    b, l, d_attn = qb.shape
    blk = MOBA_BLOCK
    nb = l // blk
    assert nb % 2 == 0
    width = V7X_MXU_DIM
    ncol = d_attn // width
    rows = (width // head_dim) * blk
    _, _, heads, _, page = cache_kt.shape
    bd, n_pages = page_table.shape
    ppb = blk // page
    n_steps = b * ncol * nb
    pps = bd * n_pages // n_steps
    assert pps * n_steps == bd * n_pages and pps % ppb == 0 and n_pages % pps == 0 and n_pages // ppb <= V7X_LANES
    spp = n_pages // pps
    pages = dict(layer=layer, pages_per_step=pps, pages_per_block=ppb, steps=spp, inv_blk=1.0 / blk)
    body = functools.partial(_moba_prompt_kernel, nb=nb, head_dim=head_dim, blk=blk, pages=pages)
    flat = lambda bi, c, i: (bi * ncol + c) * nb + i
    grid_spec = pltpu.PrefetchScalarGridSpec(
        num_scalar_prefetch=1,
        grid=(b, ncol, nb),
        in_specs=[pl.BlockSpec((None, blk, width), lambda bi, c, i, pt: (bi, i, c)),
                  pl.BlockSpec((None, l, width), lambda bi, c, i, pt: (bi, 0, c)),
                  pl.BlockSpec((None, l, width), lambda bi, c, i, pt: (bi, 0, c)),
                  pl.BlockSpec((None, nb, width), lambda bi, c, i, pt: (bi, 0, c)),
                  pl.BlockSpec(memory_space=pl.ANY)],
        out_specs=[pl.BlockSpec((None, blk, width), lambda bi, c, i, pt: (bi, i, c)),
                   pl.BlockSpec((None, heads, head_dim, V7X_LANES), lambda bi, c, i, pt: (flat(bi, c, i) // spp, 0, 0, 0))],
        scratch_shapes=[pltpu.VMEM((width // V7X_LANES, l, 2 * V7X_LANES), _BF16),
                        pltpu.VMEM((rows, V7X_LANES), _F32),
                        pltpu.VMEM((rows, V7X_LANES), _F32),
                        pltpu.VMEM((rows, width), _F32),
                        pltpu.VMEM((2, pps, heads, head_dim, page), _F32),
                        pltpu.SemaphoreType.DMA((2,))],
    )
    return pl.pallas_call(
        body,
        grid_spec=grid_spec,
        out_shape=[jax.ShapeDtypeStruct((b, l, d_attn), _BF16),
                   jax.ShapeDtypeStruct((bd, heads, head_dim, V7X_LANES), _F32)],
        compiler_params=_cparams("arbitrary", "arbitrary", "arbitrary"),
        name="moba_prompt",
    )(page_table, qb, kb, vb, kmean, cache_kt)


def _select_kernel(qh_ref, kmt_ref, sel_ref, *, nbp):
    rows = qh_ref.shape[0]
    gate = _dot(qh_ref[...], kmt_ref[...].astype(_BF16))
    n_iota = lax.broadcasted_iota(jnp.int32, (rows, V7X_LANES), 1)
    gate = jnp.where(n_iota < nbp, gate, -jnp.inf)
    rank = jnp.zeros((rows, V7X_LANES), jnp.int32)
    for m in range(nbp):
        gm = gate[:, m:m + 1]
        tie_first = jnp.where(m < n_iota, 1, 0)
        rank = rank + jnp.where(gm > gate, 1, jnp.where(gm == gate, tie_first, 0))
    out = jnp.zeros((rows, V7X_LANES), jnp.int32)
    for j in range(MOBA_TOPK):
        hit = jnp.where(n_iota < nbp, jnp.where(rank == j, n_iota, 0), 0)
        out = jnp.where(n_iota == j, jnp.sum(hit, axis=1, keepdims=True), out)
    sel_ref[...] = out


def _select_call(q_s, kmean_t, *, heads, head_dim, nbp):
    bd, s_len, d_attn = q_s.shape
    rows = heads * s_len
    head_of_lane = jnp.arange(d_attn) // head_dim
    qh = jnp.where(head_of_lane[None, None, None, :] == jnp.arange(heads)[None, :, None, None],
                   q_s[:, None, :, :], jnp.zeros((), q_s.dtype)).reshape(bd, rows, d_attn)
    return pl.pallas_call(
        functools.partial(_select_kernel, nbp=nbp),
        grid=(bd,),
        in_specs=[pl.BlockSpec((None, rows, d_attn), lambda b: (b, 0, 0)),
                  pl.BlockSpec((None, d_attn, V7X_LANES), lambda b: (b, 0, 0))],
        out_specs=pl.BlockSpec((None, rows, V7X_LANES), lambda b: (b, 0, 0)),
        out_shape=jax.ShapeDtypeStruct((bd, rows, V7X_LANES), jnp.int32),
        compiler_params=_cparams("parallel"),
        name="moba_select",
    )(qh, kmean_t)


def _sample_attend_kernel(pt_ref, sel_ref, q_ref, kn_ref, vn_ref, ck_ref, cv_ref, o_ref, kbuf, vbuf, sem,
                          *, layer, heads, s_len, page, ppb):
    b = pl.program_id(0)
    n_steps = pl.num_programs(0)
    n_sel = s_len * MOBA_TOPK
    blk = page * ppb

    def copies(seq, buf):
        out = []
        for h in range(heads):
            for slot in range(n_sel):
                blk_idx = sel_ref[(seq * heads + h) * n_sel + slot]
                for r in range(ppb):
                    pg = pt_ref[seq, blk_idx * ppb + r]
                    dst = pl.ds((slot * ppb + r) * page, page)
                    out.append(pltpu.make_async_copy(ck_ref.at[layer, pg, h], kbuf.at[buf, h, :, dst], sem.at[0, buf]))
                    out.append(pltpu.make_async_copy(cv_ref.at[layer, pg, h], vbuf.at[buf, h, :, dst], sem.at[1, buf]))
        return out

    def attend(buf):
        for h in range(heads):
            q = q_ref[h]
            s_past = _dot(q, kbuf[buf, h].astype(_BF16))
            qi = lax.broadcasted_iota(jnp.int32, s_past.shape, 0)
            slot_q = lax.broadcasted_iota(jnp.int32, s_past.shape, 1) // (blk * MOBA_TOPK)
            s_past = jnp.where(qi == slot_q, s_past, MASKED_LOGIT)
            s_own = _dot_nt(q, kn_ref[h])
            causal = (lax.broadcasted_iota(jnp.int32, s_own.shape, 1)
                      <= lax.broadcasted_iota(jnp.int32, s_own.shape, 0))
            s_own = jnp.where(causal, s_own, MASKED_LOGIT)
            m = jnp.maximum(jnp.max(s_past, axis=1, keepdims=True), jnp.max(s_own, axis=1, keepdims=True))
            p_past = jnp.exp2(s_past - m)
            p_own = jnp.exp2(s_own - m)
            denom = jnp.sum(p_past, axis=1, keepdims=True) + jnp.sum(p_own, axis=1, keepdims=True)
            o = _dot_nt(p_past.astype(_BF16), vbuf[buf, h].astype(_BF16)) + _dot(p_own.astype(_BF16), vn_ref[h])
            o_ref[h] = (o / denom).astype(o_ref.dtype)

    @pl.when(b == 0)
    def _():
        for cp in copies(0, 0):
            cp.start()

    for parity in range(2):
        @pl.when(b % 2 == parity)
        def _(parity=parity):
            @pl.when(b + 1 < n_steps)
            def _():
                for cp in copies(b + 1, 1 - parity):
                    cp.start()
            for cp in copies(b, parity):
                cp.wait()
            attend(parity)


def _sample_attend_call(page_table, sel, q_h, kn_h, vn_h, cache_kt, cache_vt, layer):
    bd, heads, s_len, head_dim = q_h.shape
    page = cache_kt.shape[-1]
    ppb = MOBA_BLOCK // page
    n_cols = s_len * MOBA_TOPK * MOBA_BLOCK
    body = functools.partial(_sample_attend_kernel, layer=layer, heads=heads, s_len=s_len, page=page, ppb=ppb)
    per_seq = pl.BlockSpec((None, heads, s_len, head_dim), lambda b, pt, sl: (b, 0, 0, 0))
    grid_spec = pltpu.PrefetchScalarGridSpec(
        num_scalar_prefetch=2,
        grid=(bd,),
        in_specs=[per_seq, per_seq, per_seq,
                  pl.BlockSpec(memory_space=pl.ANY), pl.BlockSpec(memory_space=pl.ANY)],
        out_specs=per_seq,
        scratch_shapes=[pltpu.VMEM((2, heads, head_dim, n_cols), _F32),
                        pltpu.VMEM((2, heads, head_dim, n_cols), _F32),
                        pltpu.SemaphoreType.DMA((2, 2))],
    )
    return pl.pallas_call(
        body,
        grid_spec=grid_spec,
        out_shape=jax.ShapeDtypeStruct((bd, heads, s_len, head_dim), _BF16),
        compiler_params=_cparams("arbitrary"),
        name="moba_sample",
    )(page_table, sel, q_h, kn_h, vn_h, cache_kt, cache_vt)


def _ssm_prep_kernel(lr_ref, li_ref, ldt_ref, brt_ref, bit_ref, cr_ref, ci_ref,
                     m_ref, ber_ref, bei_ref, cpr_ref, cpi_ref, aer_ref, aei_ref, *, tok_variants):
    tc = SSM_CHUNK
    hg = cr_ref.shape[0]
    dt = jnp.exp(ldt_ref[...])
    lr, li = lr_ref[...], li_ref[...]
    mag = jnp.exp(lr * dt)
    a_re, a_im = mag * jnp.cos(li * dt), mag * jnp.sin(li * dt)
    den = lr * lr + li * li
    f_re = ((a_re - 1.0) * lr + a_im * li) / den
    f_im = (a_im * lr - (a_re - 1.0) * li) / den
    brt, bit = brt_ref[...], bit_ref[...]
    bb_re = f_re * brt - f_im * bit
    bb_im = f_re * bit + f_im * brt
    cr, ci = cr_ref[...], ci_ref[...]

    pr, pi = jnp.ones_like(a_re), jnp.zeros_like(a_re)
    ca_re, ca_im, bp_re, bp_im, ap_re, ap_im = [], [], [], [], [], []
    for _ in range(tc + 1):
        ca_re.append(cr * pr - ci * pi)
        ca_im.append(cr * pi + ci * pr)
        bp_re.append(pr * bb_re - pi * bb_im)
        bp_im.append(pr * bb_im + pi * bb_re)
        ap_re.append(pr)
        ap_im.append(pi)
        pr, pi = pr * a_re - pi * a_im, pr * a_im + pi * a_re

    hi = lax.Precision.HIGHEST
    nt = (((1,), (1,)), ((), ()))
    lag_rows = lambda parts: jnp.concatenate(parts[:tc], axis=0)
    base = (lax.dot_general(bb_re, lag_rows(ca_re), nt, precision=hi, preferred_element_type=_F32)
            - lax.dot_general(bb_im, lag_rows(ca_im), nt, precision=hi, preferred_element_type=_F32))
    lane = lax.broadcasted_iota(jnp.int32, base.shape, 1)
    for s in range(tc):
        shifted = pltpu.roll(base, s * hg, 1) if s else base
        m_ref[s * hg:(s + 1) * hg, :] = jnp.where(lane >= s * hg, shifted, 0.0).astype(m_ref.dtype)

    zero = jnp.zeros_like(bb_re)
    for v, n in enumerate(tok_variants):
        ber_ref[v] = jnp.concatenate([bp_re[n - 1 - s] if s < n else zero for s in range(tc)],
                                     axis=0).astype(ber_ref.dtype)
        bei_ref[v] = jnp.concatenate([bp_im[n - 1 - s] if s < n else zero for s in range(tc)],
                                     axis=0).astype(bei_ref.dtype)
        aer_ref[v] = ap_re[n]
        aei_ref[v] = ap_im[n]
    cpr_ref[...] = jnp.concatenate(ca_re[1:], axis=0).astype(cpr_ref.dtype)
    cpi_ref[...] = (-jnp.concatenate(ca_im[1:], axis=0)).astype(cpi_ref.dtype)


def _ssm_prep_call(lam_re, lam_im, log_dt, b_re, b_im, c_re, c_im, *, tok_variants):
    g, p = lam_re.shape
    hg = b_re.shape[-1]
    w = SSM_CHUNK * hg
    nv = len(tok_variants)
    body = functools.partial(_ssm_prep_kernel, tok_variants=tok_variants)

    def full(*dims):
        return pl.BlockSpec((None,) + dims, lambda i: (i,) + (0,) * len(dims))

    return pl.pallas_call(
        body,
        grid=(g,),
        in_specs=[full(1, p), full(1, p), full(1, 1), full(hg, p), full(hg, p), full(hg, p), full(hg, p)],
        out_specs=[full(w, w), full(nv, w, p), full(nv, w, p), full(w, p), full(w, p), full(nv, 1, p), full(nv, 1, p)],
        out_shape=[jax.ShapeDtypeStruct((g, w, w), _BF16),
                   jax.ShapeDtypeStruct((g, nv, w, p), _BF16),
                   jax.ShapeDtypeStruct((g, nv, w, p), _BF16),
                   jax.ShapeDtypeStruct((g, w, p), _BF16),
                   jax.ShapeDtypeStruct((g, w, p), _BF16),
                   jax.ShapeDtypeStruct((g, nv, 1, p), _F32),
                   jax.ShapeDtypeStruct((g, nv, 1, p), _F32)],
        compiler_params=_cparams("parallel"),
        name="ssm_prep",
    )(lam_re[:, None, :], lam_im[:, None, :], log_dt[:, None, None],
      b_re.transpose(0, 2, 1), b_im.transpose(0, 2, 1), c_re, c_im)


def _ssm_tables(prep, d_skip, variant):
    m_intra, bend_re, bend_im, cp_re, cp_im, aend_re, aend_im = prep
    dvec = jnp.tile(d_skip[:, None, :], (1, SSM_CHUNK, 1)).reshape(d_skip.shape[0], 1, -1)
    return (m_intra, bend_re[:, variant], bend_im[:, variant], cp_re, cp_im,
            aend_re[:, variant], aend_im[:, variant], dvec)


def _pow_static(z_re, z_im, n):
    out = None
    while n:
        if n & 1:
            out = (z_re, z_im) if out is None else (out[0] * z_re - out[1] * z_im, out[0] * z_im + out[1] * z_re)
        n >>= 1
        if n:
            z_re, z_im = z_re * z_re - z_im * z_im, 2.0 * z_re * z_im
    return out


def _ssm_kernel(x_ref, mt_ref, ber_ref, bei_ref, cpr_ref, cpi_ref, aer_ref, aei_ref, dv_ref,
                h0r_ref, h0i_ref, y_ref, hr_ref, hi_ref, sr, si, hsr, hsi, *, gb, nb, nc, nseg):
    for g in range(gb):
        xb = x_ref[g].astype(_BF16)
        sr[g] = _dot(xb, ber_ref[g])
        si[g] = _dot(xb, bei_ref[g])

    a_re, a_im = aer_ref[...], aei_ref[...]
    cmul = lambda xr, xi, yr, yi: (xr * yr - xi * yi, xr * yi + xi * yr)
    if nc == 1:
        h_re, h_im = h0r_ref[...], h0i_ref[...]
        hsr[...] = h_re
        hsi[...] = h_im
        d_re, d_im = cmul(a_re, a_im, h_re, h_im)
        hr_ref[...] = d_re + sr[...]
        hi_ref[...] = d_im + si[...]
    else:
        cps = nc // nseg
        tile = lambda b, jj: pl.ds(pl.multiple_of((b * cps + jj) * nseg, nseg), nseg)

        def local_step(jj, carry):
            nxt = []
            for b in range(nb):
                h_re, h_im = carry[b]
                rows = tile(b, jj)
                hsr[:, rows, :] = h_re
                hsi[:, rows, :] = h_im
                d_re, d_im = cmul(a_re, a_im, h_re, h_im)
                nxt.append((d_re + sr[:, rows, :], d_im + si[:, rows, :]))
            return tuple(nxt)

        zero = jnp.zeros((gb, nseg, a_re.shape[-1]), _F32)
        seg_end = lax.fori_loop(0, cps, local_step, tuple((zero, zero) for _ in range(nb)))

        s_re, s_im = _pow_static(a_re, a_im, cps)
        seg_id = lax.broadcasted_iota(jnp.int32, zero.shape, 1)
        starts = []
        for b in range(nb):
            c_re, c_im = h0r_ref[:, b:b + 1, :], h0i_ref[:, b:b + 1, :]
            st_re, st_im = zero, zero
            for sgm in range(nseg):
                st_re = jnp.where(seg_id == sgm, c_re, st_re)
                st_im = jnp.where(seg_id == sgm, c_im, st_im)
                d_re, d_im = cmul(s_re, s_im, c_re, c_im)
                c_re = d_re + seg_end[b][0][:, sgm:sgm + 1, :]
                c_im = d_im + seg_end[b][1][:, sgm:sgm + 1, :]
            starts.append((st_re, st_im))
            hr_ref[:, b:b + 1, :] = c_re
            hi_ref[:, b:b + 1, :] = c_im

        def fix_step(jj, carry):
            p_re, p_im = carry
            for b in range(nb):
                rows = tile(b, jj)
                d_re, d_im = cmul(p_re, p_im, starts[b][0], starts[b][1])
                hsr[:, rows, :] = hsr[:, rows, :] + d_re
                hsi[:, rows, :] = hsi[:, rows, :] + d_im
            return cmul(p_re, p_im, a_re, a_im)

        lax.fori_loop(0, cps, fix_step, (jnp.ones_like(a_re), jnp.zeros_like(a_re)))

    for g in range(gb):
        x = x_ref[g]
        y = _dot(x.astype(_BF16), mt_ref[g])
        y = y + _dot_nt(hsr[g].astype(_BF16), cpr_ref[g]) + _dot_nt(hsi[g].astype(_BF16), cpi_ref[g])
        y_ref[g] = y + x * dv_ref[g]


def _ssm_call(xg, tables, h0_re, h0_im, *, nb, nc, nseg, gb):
    m_intra, bend_re, bend_im, cp_re, cp_im, aend_re, aend_im, dvec = tables
    g, n, w = xg.shape
    p = bend_re.shape[-1]
    assert n == nb * nc and nc % nseg == 0
    body = functools.partial(_ssm_kernel, gb=gb, nb=nb, nc=nc, nseg=nseg)

    def spec(*dims):
        return pl.BlockSpec((gb,) + dims, lambda i: (i,) + (0,) * len(dims))

    return pl.pallas_call(
        body,
        grid=(g // gb,),
        in_specs=[spec(n, w), spec(w, w), spec(w, p), spec(w, p), spec(w, p), spec(w, p),
                  spec(1, p), spec(1, p), spec(1, w), spec(nb, p), spec(nb, p)],
        out_specs=[spec(n, w), spec(nb, p), spec(nb, p)],
        out_shape=[jax.ShapeDtypeStruct((g, n, w), _F32),
                   jax.ShapeDtypeStruct((g, nb, p), _F32),
                   jax.ShapeDtypeStruct((g, nb, p), _F32)],
        scratch_shapes=[pltpu.VMEM((gb, n, p), _F32)] * 4,
        compiler_params=_cparams("parallel"),
        name="ssm",
    )(xg, m_intra, bend_re, bend_im, cp_re, cp_im, aend_re, aend_im, dvec, h0_re, h0_im)


def _merge_kernel(x_ref, oa_ref, y_ref, g_ref, wg_ref, wglu_ref, wpa_ref, wpb_ref, wout_ref, fg_ref,
                  o_ref, slab_ref, *, d_attn, d_ssm, hg, tok, final_norm):
    x = x_ref[...]
    d = x.shape[-1]
    hb = _rmsnorm(x, g_ref[...]).astype(_BF16)
    z_a = _dot(hb, wg_ref[:, 0:d_attn])
    ya = oa_ref[...].astype(_F32) * jax.nn.silu(z_a)
    br_a = _dot(ya.astype(_BF16), wpa_ref[...])
    ys = _chunks_to_tokens(y_ref, slab_ref, hg=hg, tok=tok)
    glu = _dot(jax.nn.gelu(ys).astype(_BF16), wglu_ref[...])
    z_b = _dot(hb, wg_ref[:, d_attn:d_attn + d_ssm])
    yb = glu[:, :d_ssm] * jax.nn.sigmoid(glu[:, d_ssm:]) * jax.nn.silu(z_b)
    br_b = _dot(yb.astype(_BF16), wpb_ref[...])
    g_a = _dot(hb, wg_ref[:, d_attn + d_ssm:d_attn + d_ssm + d])
    g_b = _dot(hb, wg_ref[:, d_attn + d_ssm + d:d_attn + d_ssm + 2 * d])
    merged = jax.nn.sigmoid(g_a) * br_a + jax.nn.sigmoid(g_b) * br_b
    out = x + _dot(merged.astype(_BF16), wout_ref[...])
    if final_norm:
        out = _rmsnorm(out, fg_ref[...])
    o_ref[...] = out


def _merge_call(x2d, oa, yg, g, w_g, w_glu, w_pa, w_pb, w_out, final_g, *, hg, tok, tm, final_norm, nseg=1):
    t, d = x2d.shape
    d_attn = oa.shape[1]
    n_groups, n_rows, w = yg.shape
    d_ssm = n_groups * hg
    yg = yg.reshape(n_groups, n_rows // nseg, nseg * w)
    row = lambda i: (i, 0)
    const = lambda a: pl.BlockSpec(a.shape, lambda i: (0, 0), pipeline_mode=pl.Buffered(1))
    body = functools.partial(_merge_kernel, d_attn=d_attn, d_ssm=d_ssm, hg=hg, tok=tok, final_norm=final_norm)
    return pl.pallas_call(
        body,
        grid=(t // tm,),
        in_specs=[pl.BlockSpec((tm, d), row), pl.BlockSpec((tm, d_attn), row),
                  pl.BlockSpec((n_groups, tm // tok, w), lambda i: (0, i // nseg, i % nseg)),
                  const(g), const(w_g), const(w_glu), const(w_pa), const(w_pb), const(w_out), const(final_g)],
        out_specs=pl.BlockSpec((tm, d), row),
        out_shape=jax.ShapeDtypeStruct((t, d), _F32),
        scratch_shapes=[pltpu.VMEM((d_ssm // V7X_LANES, tm, V7X_LANES), _F32)],
        compiler_params=_cparams("parallel"),
        name="merge",
    )(x2d, oa, yg, g, w_g, w_glu, w_pa, w_pb, w_out, final_g)


def kernel(x_prompt, x_sample, cache_k, cache_v, state_ssm_re, state_ssm_im, page_table, norm_g, w_in,
           lam_re, lam_im, log_dt, b_re, b_im, c_re, c_im, d_skip, w_glu, w_pa, w_pb, w_out, final_norm_g):
    b, l, d = x_prompt.shape
    bd, s_len, _ = x_sample.shape
    depth = norm_g.shape[0]
    _, _, page, heads, head_dim = cache_k.shape
    n_groups, p_state = lam_re.shape[1:]
    hg = b_re.shape[-1]
    d_attn, d_ssm = heads * head_dim, n_groups * hg
    n_pages = page_table.shape[1]
    nbp = n_pages * page // MOBA_BLOCK
    assert w_in.shape[-1] == 4 * d_attn + 2 * d_ssm + 2 * d
    assert l % MOBA_BLOCK == 0 and MOBA_BLOCK % page == 0 and (n_pages * page) % MOBA_BLOCK == 0
    assert s_len <= SSM_CHUNK and SSM_CHUNK * hg == V7X_MXU_DIM and l % SSM_CHUNK == 0
    assert V7X_LANES % hg == 0 and d_ssm % V7X_LANES == 0 and d_attn % V7X_MXU_DIM == 0

    nseg = V7X_SUBLANES
    tm_p = l // nseg
    assert l % nseg == 0 and tm_p % MOBA_BLOCK == 0 and tm_p % (SSM_CHUNK * V7X_SUBLANES) == 0
    tm_s = bd * s_len
    xp = x_prompt.reshape(b * l, d)
    xs = x_sample.reshape(bd * s_len, d)
    cache_kt = cache_k.transpose(0, 1, 3, 4, 2)
    cache_vt = cache_v.transpose(0, 1, 3, 4, 2)
    zeros_state = jnp.zeros((n_groups, b, p_state), _F32)
    fg = final_norm_g[None, :]
    proj_kw = dict(d_attn=d_attn, d_ssm=d_ssm, head_dim=head_dim, hg=hg)

    kp, vp, ks, vs, hpr, hpi, hsr, hsi = ([] for _ in range(8))
    for layer in range(depth):
        wl = w_in[layer]
        o_z, o_u, o_zb, o_ga = 3 * d_attn, 4 * d_attn, 4 * d_attn + d_ssm, 4 * d_attn + 2 * d_ssm
        w_a = jnp.concatenate([wl[:, :o_z], wl[:, o_u:o_zb]], axis=1).astype(_BF16)
        w_g = jnp.concatenate([wl[:, o_z:o_u], wl[:, o_zb:o_ga], wl[:, o_ga:]], axis=1).astype(_BF16)
        gl = norm_g[layer][None, :]
        merge_w = (gl, w_g, w_glu[layer].astype(_BF16), w_pa[layer].astype(_BF16),
                   w_pb[layer].astype(_BF16), w_out[layer].astype(_BF16), fg)
        last = layer == depth - 1
        prep = _ssm_prep_call(lam_re[layer], lam_im[layer], log_dt[layer], b_re[layer], b_im[layer],
                              c_re[layer], c_im[layer], tok_variants=(SSM_CHUNK, s_len))

        qb, k, v, kb, vb, xg, kmean = _proj_call(xp, gl, w_a, tok=SSM_CHUNK, tm=tm_p, with_kmean=True, seq_len=l,
                                                 **proj_kw)
        oa, kmean_t = _moba_prompt_call(qb.reshape(b, l, d_attn), kb.reshape(b, l, d_attn), vb.reshape(b, l, d_attn),
                                        kmean.reshape(b, l // MOBA_BLOCK, d_attn), cache_kt, page_table, layer,
                                        head_dim=head_dim)
        yg, h_re, h_im = _ssm_call(xg.reshape(n_groups, -1, SSM_CHUNK * hg), _ssm_tables(prep, d_skip[layer], 0),
                                   zeros_state, zeros_state, nb=b, nc=l // SSM_CHUNK, nseg=nseg, gb=4)
        xp = _merge_call(xp, oa.reshape(b * l, d_attn), yg, *merge_w, hg=hg, tok=SSM_CHUNK, tm=tm_p,
                         final_norm=last, nseg=nseg)
        kp.append(k.reshape(b, heads, head_dim, l).transpose(0, 3, 1, 2))
        vp.append(v.reshape(b, heads, head_dim, l).transpose(0, 3, 1, 2))
        hpr.append(h_re.transpose(1, 0, 2))
        hpi.append(h_im.transpose(1, 0, 2))

        qb, k, v, kb, vb, xg = _proj_call(xs, gl, w_a, tok=s_len, tm=tm_s, with_kmean=False, **proj_kw)
        sel = _select_call(qb.reshape(bd, s_len, d_attn), kmean_t.reshape(bd, d_attn, V7X_LANES),
                           heads=heads, head_dim=head_dim, nbp=nbp)
        sel = sel[:, :, :MOBA_TOPK].reshape(-1)
        to_heads = lambda a: a.reshape(bd, s_len, heads, head_dim).transpose(0, 2, 1, 3)
        oa = _sample_attend_call(page_table, sel, to_heads(qb), to_heads(kb), to_heads(vb),
                                 cache_kt, cache_vt, layer)
        oa = oa.transpose(0, 2, 1, 3).reshape(bd * s_len, d_attn)
        yg, h_re, h_im = _ssm_call(xg, _ssm_tables(prep, d_skip[layer], 1),
                                   state_ssm_re[layer].transpose(1, 0, 2), state_ssm_im[layer].transpose(1, 0, 2),
                                   nb=bd, nc=1, nseg=1, gb=4)
        xs = _merge_call(xs, oa, yg, *merge_w, hg=hg, tok=s_len, tm=tm_s, final_norm=last)
        ks.append(k.reshape(bd, s_len, heads, head_dim))
        vs.append(v.reshape(bd, s_len, heads, head_dim))
        hsr.append(h_re.transpose(1, 0, 2))
        hsi.append(h_im.transpose(1, 0, 2))

    return (xp.reshape(b, l, d), xs.reshape(bd, s_len, d),
            jnp.stack(kp), jnp.stack(vp), jnp.stack(ks), jnp.stack(vs),
            jnp.stack(hpr), jnp.stack(hpi), jnp.stack(hsr), jnp.stack(hsi))
```

```python
import functools

import jax
import jax.numpy as jnp
from jax import lax
from jax.experimental import pallas as pl
from jax.experimental.pallas import tpu as pltpu

MOBA_BLOCK = 256
MOBA_TOPK = 3
RMS_EPS = 1e-6

V7X_LANES = 128
V7X_SUBLANES = 8
V7X_MXU_DIM = 256
V7X_VMEM_LIMIT_BYTES = 56 * 1024 * 1024

SSM_CHUNK = 16
MASKED_LOGIT = -1e30
LOG2_E = 1.4426950408889634

_F32 = jnp.float32
_BF16 = jnp.bfloat16


def _cparams(*sem):
    return pltpu.CompilerParams(dimension_semantics=sem, vmem_limit_bytes=V7X_VMEM_LIMIT_BYTES)


def _rmsnorm(x, g):
    return x * lax.rsqrt(jnp.mean(x * x, axis=-1, keepdims=True) + RMS_EPS) * g


def _dot(a, b):
    return jnp.dot(a, b, preferred_element_type=_F32)


def _dot_nt(a, b):
    return lax.dot_general(a, b, (((1,), (1,)), ((), ())), preferred_element_type=_F32)


def _twice(a):
    return jnp.concatenate([a, a], axis=1)


def _tokens_to_chunks(u, slab_ref, xg_ref, *, hg, tok):
    tm, c = u.shape
    n = tm // tok
    gpl = V7X_LANES // hg
    lane_slot = lax.broadcasted_iota(jnp.int32, (n, V7X_LANES), 1) // hg
    for cb in range(c // V7X_LANES):
        slab_ref[cb] = u[:, cb * V7X_LANES:(cb + 1) * V7X_LANES]
    for cb in range(c // V7X_LANES):
        rows = [slab_ref[cb, pl.ds(t, n, stride=tok), :] for t in range(tok)]
        rolled = {}
        for gi in range(gpl):
            cols = []
            for cv in range(SSM_CHUNK // gpl):
                acc = jnp.zeros((n, V7X_LANES), _F32)
                for jj in range(gpl):
                    t = cv * gpl + jj
                    if t >= tok:
                        continue
                    shift = ((jj - gi) * hg) % V7X_LANES
                    if (t, shift) not in rolled:
                        rolled[(t, shift)] = pltpu.roll(rows[t], shift, 1) if shift else rows[t]
                    acc = jnp.where(lane_slot == jj, rolled[(t, shift)], acc)
                cols.append(acc)
            xg_ref[cb * gpl + gi] = jnp.concatenate(cols, axis=1)


def _chunks_to_tokens(y_ref, slab_ref, *, hg, tok):
    g, n, _ = y_ref.shape
    gpl = V7X_LANES // hg
    lane_slot = lax.broadcasted_iota(jnp.int32, (n, V7X_LANES), 1) // hg
    n_slabs = g // gpl
    for cb in range(n_slabs):
        for cv in range(SSM_CHUNK // gpl):
            src = [y_ref[cb * gpl + gi, :, cv * V7X_LANES:(cv + 1) * V7X_LANES] for gi in range(gpl)]
            rolled = {}
            for jj in range(gpl):
                t = cv * gpl + jj
                if t >= tok:
                    continue
                piece = jnp.zeros((n, V7X_LANES), _F32)
                for gi in range(gpl):
                    shift = ((gi - jj) * hg) % V7X_LANES
                    if (gi, shift) not in rolled:
                        rolled[(gi, shift)] = pltpu.roll(src[gi], shift, 1) if shift else src[gi]
                    piece = jnp.where(lane_slot == gi, rolled[(gi, shift)], piece)
                slab_ref[cb, pl.ds(t, n, stride=tok), :] = piece
    return jnp.concatenate([slab_ref[cb] for cb in range(n_slabs)], axis=1)


def _proj_kernel(x_ref, g_ref, w_ref, q_ref, k_ref, v_ref, kb_ref, vb_ref, xg_ref, km_ref, slab_ref,
                 *, d_attn, d_ssm, head_dim, blk, hg, tok, kv_token_minor):
    hb = _rmsnorm(x_ref[...], g_ref[...]).astype(_BF16)
    q = _dot(hb, w_ref[:, 0:d_attn])
    k = _dot(hb, w_ref[:, d_attn:2 * d_attn])
    v = _dot(hb, w_ref[:, 2 * d_attn:3 * d_attn])
    u = _dot(hb, w_ref[:, 3 * d_attn:3 * d_attn + d_ssm])
    q_ref[...] = (q * (head_dim ** -0.5 * LOG2_E)).astype(_BF16)
    k_ref[...] = k.T if kv_token_minor else k
    v_ref[...] = v.T if kv_token_minor else v
    kb_ref[...] = k.astype(_BF16)
    vb_ref[...] = v.astype(_BF16)
    _tokens_to_chunks(u, slab_ref, xg_ref, hg=hg, tok=tok)
    if km_ref is not None:
        tm = k.shape[0]
        km_ref[0] = jnp.sum(k.reshape(tm // blk, blk, d_attn), axis=1) * (1.0 / blk)


def _proj_call(x2d, g, w_a, *, d_attn, d_ssm, head_dim, hg, tok, tm, with_kmean, seq_len=None):
    t, d = x2d.shape
    n_tiles = t // tm
    n_groups = d_ssm // hg
    row = lambda i: (i, 0)
    w = SSM_CHUNK * hg
    if seq_len is None:
        kv_shape = jax.ShapeDtypeStruct((t, d_attn), _F32)
        kv_spec = pl.BlockSpec((tm, d_attn), row)
        xg_shape = jax.ShapeDtypeStruct((n_groups, t // tok, w), _F32)
        xg_spec = pl.BlockSpec((n_groups, tm // tok, w), lambda i: (0, i, 0))
    else:
        tps = seq_len // tm
        kv_shape = jax.ShapeDtypeStruct((t // seq_len, d_attn, seq_len), _F32)
        kv_spec = pl.BlockSpec((None, d_attn, tm), lambda i: (i // tps, 0, i % tps))
        xg_shape = jax.ShapeDtypeStruct((n_groups, t // tok // tps, tps * w), _F32)
        xg_spec = pl.BlockSpec((n_groups, tm // tok, w), lambda i: (0, i // tps, i % tps))
    out_shape = [
        jax.ShapeDtypeStruct((t, d_attn), _BF16),
        kv_shape,
        kv_shape,
        jax.ShapeDtypeStruct((t, d_attn), _BF16),
        jax.ShapeDtypeStruct((t, d_attn), _BF16),
        xg_shape,
    ]
    out_specs = [pl.BlockSpec((tm, d_attn), row), kv_spec, kv_spec,
                 pl.BlockSpec((tm, d_attn), row), pl.BlockSpec((tm, d_attn), row), xg_spec]
    body = functools.partial(_proj_kernel, d_attn=d_attn, d_ssm=d_ssm, head_dim=head_dim, blk=MOBA_BLOCK,
                             hg=hg, tok=tok, kv_token_minor=seq_len is not None)
    if with_kmean:
        out_shape.append(jax.ShapeDtypeStruct((n_tiles, tm // MOBA_BLOCK, d_attn), _F32))
        out_specs.append(pl.BlockSpec((1, tm // MOBA_BLOCK, d_attn), lambda i: (i, 0, 0)))
        kern = body
    else:
        kern = lambda *refs: body(*refs[:-1], None, refs[-1])
    return pl.pallas_call(
        kern,
        grid=(n_tiles,),
        in_specs=[pl.BlockSpec((tm, d), row),
                  pl.BlockSpec((1, d), lambda i: (0, 0)),
                  pl.BlockSpec(w_a.shape, lambda i: (0, 0))],
        out_specs=out_specs,
        out_shape=out_shape,
        scratch_shapes=[pltpu.VMEM((d_ssm // V7X_LANES, tm, V7X_LANES), _F32)],
        compiler_params=_cparams("parallel"),
        name="proj",
    )(x2d, g, w_a)


def _kmean_pages_step(pt_ref, ck_ref, out_ref, buf, sem, step, n_total, *, layer, pages_per_step, pages_per_block,
                      steps, inv_blk):
    c = step % steps
    bps = pages_per_step // pages_per_block

    def copies(st, slot):
        b = st // steps
        first = (st % steps) * pages_per_step
        return [pltpu.make_async_copy(ck_ref.at[layer, pt_ref[b, first + r]], buf.at[slot, r], sem.at[slot])
                for r in range(pages_per_step)]

    def reduce_pages(slot):
        lane = lax.broadcasted_iota(jnp.int32, out_ref.shape, 2)
        acc = out_ref[...]
        for j in range(bps):
            tot = buf[slot, j * pages_per_block]
            for r in range(1, pages_per_block):
                tot = tot + buf[slot, j * pages_per_block + r]
            mean = jnp.sum(tot, axis=-1, keepdims=True) * inv_blk
            acc = jnp.where(lane == c * bps + j, mean, acc)
        out_ref[...] = acc

    @pl.when(step == 0)
    def _():
        for cp in copies(0, 0):
            cp.start()

    @pl.when(c == 0)
    def _():
        out_ref[...] = jnp.zeros_like(out_ref)

    slot = step % 2

    @pl.when(step + 1 < n_total)
    def _():
        for cp in copies(step + 1, 1 - slot):
            cp.start()

    for cp in copies(step, slot):
        cp.wait()
    reduce_pages(slot)


def _moba_prompt_kernel(pt_ref, q_ref, k_ref, v_ref, km_ref, ck_ref, o_ref, kmt_ref, kaug_ref, m_ref, l_ref, acc_ref,
                        page_buf, page_sem, *, nb, head_dim, blk, pages):
    i = pl.program_id(2)
    step = (pl.program_id(0) * pl.num_programs(1) + pl.program_id(1)) * nb + i
    _kmean_pages_step(pt_ref, ck_ref, kmt_ref, page_buf, page_sem, step,
                      pl.num_programs(0) * pl.num_programs(1) * nb, **pages)
    width = q_ref.shape[-1]
    pw = V7X_LANES
    n_pairs = width // pw
    hpp = pw // head_dim
    prow = hpp * blk
    rows = n_pairs * prow
    seq = k_ref.shape[0]

    @pl.when(i == 0)
    def _():
        key_blk = lax.broadcasted_iota(jnp.int32, (seq, pw), 0) // blk
        onehot = jnp.where(key_blk == lax.broadcasted_iota(jnp.int32, (seq, pw), 1), 1.0, 0.0).astype(_BF16)
        for p in range(n_pairs):
            kaug_ref[p, :, 0:pw] = k_ref[:, p * pw:(p + 1) * pw]
            kaug_ref[p, :, pw:2 * pw] = onehot

    q = q_ref[...]
    lane_head = lax.broadcasted_iota(jnp.int32, (blk, pw), 1) // head_dim
    q2 = []
    for p in range(n_pairs):
        qp = q[:, p * pw:(p + 1) * pw]
        q2.append(jnp.concatenate([jnp.where(lane_head == h, qp, jnp.zeros_like(qp)) for h in range(hpp)], axis=0))

    km = km_ref[...].astype(_BF16)
    gate = jnp.concatenate([_dot_nt(km[:, p * pw:(p + 1) * pw], q2[p]) for p in range(n_pairs)], axis=1)
    n_iota = lax.broadcasted_iota(jnp.int32, (nb, rows), 0)
    gate = jnp.where(n_iota < i, gate, -jnp.inf)
    rank = jnp.zeros((nb, rows), jnp.int32)
    for m in range(nb):
        gm = gate[m:m + 1, :]
        tie_first = jnp.where(m < n_iota, 1, 0)
        rank = rank + jnp.where(gm > gate, 1, jnp.where(gm == gate, tie_first, 0))
    keep = jnp.where(n_iota < i, jnp.where(rank < MOBA_TOPK, 1, 0), jnp.where(n_iota == i, 1, 0))
    bias = jnp.where(keep == 1, 0.0, MASKED_LOGIT).astype(_F32)
    bias = jnp.concatenate([bias, jnp.full((pw - nb, rows), MASKED_LOGIT, _F32)], axis=0).T
    qaug = [jnp.concatenate([q2[p], bias[p * prow:(p + 1) * prow].astype(_BF16)], axis=1) for p in range(n_pairs)]

    span = 2 * blk
    n_chunks = span // pw

    def logits(kstart):
        return jnp.concatenate([_dot_nt(qaug[p], kaug_ref[p, pl.ds(kstart, span), :]) for p in range(n_pairs)],
                               axis=0)

    def lane_chunks(a):
        return [a[:, c * pw:(c + 1) * pw] for c in range(n_chunks)]

    def row_max(s):
        return jnp.broadcast_to(jnp.max(functools.reduce(jnp.maximum, lane_chunks(s)), axis=1, keepdims=True),
                                (rows, pw))

    def widen(a, n):
        return jnp.concatenate([a] * n, axis=1)

    own_span = i // 2
    start = pl.multiple_of(own_span * span, span)
    s = logits(start)
    q_off = lax.broadcasted_iota(jnp.int32, (rows, span), 0) % blk + (i - 2 * own_span) * blk
    k_off = lax.broadcasted_iota(jnp.int32, (rows, span), 1)
    s = jnp.where(k_off <= q_off, s, MASKED_LOGIT)
    m0 = row_max(s)
    p0 = jnp.exp2(s - widen(m0, n_chunks))
    m_ref[...] = m0
    l_ref[...] = functools.reduce(jnp.add, lane_chunks(p0))
    acc_ref[...] = _dot(p0.astype(_BF16), v_ref[pl.ds(start, span), :])

    for n in range(nb // 2 - 1):
        @pl.when(n < own_span)
        def _(n=n):
            s = logits(n * span)
            m_old = m_ref[...]
            m_new = jnp.maximum(m_old, row_max(s))
            alpha = jnp.exp2(m_old - m_new)
            p = jnp.exp2(s - widen(m_new, n_chunks))
            l_ref[...] = alpha * l_ref[...] + functools.reduce(jnp.add, lane_chunks(p))
            acc_ref[...] = _twice(alpha) * acc_ref[...] + _dot(p.astype(_BF16), v_ref[n * span:(n + 1) * span, :])
            m_ref[...] = m_new

    o4 = acc_ref[...] / jnp.sum(l_ref[...], axis=1, keepdims=True)
    lane_head4 = lax.broadcasted_iota(jnp.int32, (blk, width), 1) // head_dim
    out = jnp.zeros((blk, width), _F32)
    for h in range(width // head_dim):
        out = out + jnp.where(lane_head4 == h, o4[h * blk:(h + 1) * blk, :], 0.0)
    o_ref[...] = out.astype(o_ref.dtype)


def _moba_prompt_call(qb, kb, vb, kmean, cache_kt, page_table, layer, *, head_dim):
    b, l, d_attn = qb.shape
    blk = MOBA_BLOCK
    nb = l // blk
    assert nb % 2 == 0
    width = V7X_MXU_DIM
    ncol = d_attn // width
    rows = (width // head_dim) * blk
    _, _, heads, _, page = cache_kt.shape
    bd, n_pages = page_table.shape
    ppb = blk // page
    n_steps = b * ncol * nb
    pps = bd * n_pages // n_steps
    assert pps * n_steps == bd * n_pages and pps % ppb == 0 and n_pages % pps == 0 and n_pages // ppb <= V7X_LANES
    spp = n_pages // pps
    pages = dict(layer=layer, pages_per_step=pps, pages_per_block=ppb, steps=spp, inv_blk=1.0 / blk)
    body = functools.partial(_moba_prompt_kernel, nb=nb, head_dim=head_dim, blk=blk, pages=pages)
    flat = lambda bi, c, i: (bi * ncol + c) * nb + i
    grid_spec = pltpu.PrefetchScalarGridSpec(
        num_scalar_prefetch=1,
        grid=(b, ncol, nb),
        in_specs=[pl.BlockSpec((None, blk, width), lambda bi, c, i, pt: (bi, i, c)),
                  pl.BlockSpec((None, l, width), lambda bi, c, i, pt: (bi, 0, c)),
                  pl.BlockSpec((None, l, width), lambda bi, c, i, pt: (bi, 0, c)),
                  pl.BlockSpec((None, nb, width), lambda bi, c, i, pt: (bi, 0, c)),
                  pl.BlockSpec(memory_space=pl.ANY)],
        out_specs=[pl.BlockSpec((None, blk, width), lambda bi, c, i, pt: (bi, i, c)),
                   pl.BlockSpec((None, heads, head_dim, V7X_LANES), lambda bi, c, i, pt: (flat(bi, c, i) // spp, 0, 0, 0))],
        scratch_shapes=[pltpu.VMEM((width // V7X_LANES, l, 2 * V7X_LANES), _BF16),
                        pltpu.VMEM((rows, V7X_LANES), _F32),
                        pltpu.VMEM((rows, V7X_LANES), _F32),
                        pltpu.VMEM((rows, width), _F32),
                        pltpu.VMEM((2, pps, heads, head_dim, page), _F32),
                        pltpu.SemaphoreType.DMA((2,))],
    )
    return pl.pallas_call(
        body,
        grid_spec=grid_spec,
        out_shape=[jax.ShapeDtypeStruct((b, l, d_attn), _BF16),
                   jax.ShapeDtypeStruct((bd, heads, head_dim, V7X_LANES), _F32)],
        compiler_params=_cparams("arbitrary", "arbitrary", "arbitrary"),
        name="moba_prompt",
    )(page_table, qb, kb, vb, kmean, cache_kt)


def _select_kernel(qh_ref, kmt_ref, sel_ref, *, nbp):
    rows = qh_ref.shape[0]
    gate = _dot(qh_ref[...], kmt_ref[...].astype(_BF16))
    n_iota = lax.broadcasted_iota(jnp.int32, (rows, V7X_LANES), 1)
    gate = jnp.where(n_iota < nbp, gate, -jnp.inf)
    rank = jnp.zeros((rows, V7X_LANES), jnp.int32)
    for m in range(nbp):
        gm = gate[:, m:m + 1]
        tie_first = jnp.where(m < n_iota, 1, 0)
        rank = rank + jnp.where(gm > gate, 1, jnp.where(gm == gate, tie_first, 0))
    out = jnp.zeros((rows, V7X_LANES), jnp.int32)
    for j in range(MOBA_TOPK):
        hit = jnp.where(n_iota < nbp, jnp.where(rank == j, n_iota, 0), 0)
        out = jnp.where(n_iota == j, jnp.sum(hit, axis=1, keepdims=True), out)
    sel_ref[...] = out


def _select_call(q_s, kmean_t, *, heads, head_dim, nbp):
    bd, s_len, d_attn = q_s.shape
    rows = heads * s_len
    head_of_lane = jnp.arange(d_attn) // head_dim
    qh = jnp.where(head_of_lane[None, None, None, :] == jnp.arange(heads)[None, :, None, None],
                   q_s[:, None, :, :], jnp.zeros((), q_s.dtype)).reshape(bd, rows, d_attn)
    return pl.pallas_call(
        functools.partial(_select_kernel, nbp=nbp),
        grid=(bd,),
        in_specs=[pl.BlockSpec((None, rows, d_attn), lambda b: (b, 0, 0)),
                  pl.BlockSpec((None, d_attn, V7X_LANES), lambda b: (b, 0, 0))],
        out_specs=pl.BlockSpec((None, rows, V7X_LANES), lambda b: (b, 0, 0)),
        out_shape=jax.ShapeDtypeStruct((bd, rows, V7X_LANES), jnp.int32),
        compiler_params=_cparams("parallel"),
        name="moba_select",
    )(qh, kmean_t)


def _sample_attend_kernel(pt_ref, sel_ref, q_ref, kn_ref, vn_ref, ck_ref, cv_ref, o_ref, kbuf, vbuf, sem,
                          *, layer, heads, s_len, page, ppb):
    b = pl.program_id(0)
    n_steps = pl.num_programs(0)
    n_sel = s_len * MOBA_TOPK
    blk = page * ppb

    def copies(seq, buf):
        out = []
        for h in range(heads):
            for slot in range(n_sel):
                blk_idx = sel_ref[(seq * heads + h) * n_sel + slot]
                for r in range(ppb):
                    pg = pt_ref[seq, blk_idx * ppb + r]
                    dst = pl.ds((slot * ppb + r) * page, page)
                    out.append(pltpu.make_async_copy(ck_ref.at[layer, pg, h], kbuf.at[buf, h, :, dst], sem.at[0, buf]))
                    out.append(pltpu.make_async_copy(cv_ref.at[layer, pg, h], vbuf.at[buf, h, :, dst], sem.at[1, buf]))
        return out

    def attend(buf):
        for h in range(heads):
            q = q_ref[h]
            s_past = _dot(q, kbuf[buf, h].astype(_BF16))
            qi = lax.broadcasted_iota(jnp.int32, s_past.shape, 0)
            slot_q = lax.broadcasted_iota(jnp.int32, s_past.shape, 1) // (blk * MOBA_TOPK)
            s_past = jnp.where(qi == slot_q, s_past, MASKED_LOGIT)
            s_own = _dot_nt(q, kn_ref[h])
            causal = (lax.broadcasted_iota(jnp.int32, s_own.shape, 1)
                      <= lax.broadcasted_iota(jnp.int32, s_own.shape, 0))
            s_own = jnp.where(causal, s_own, MASKED_LOGIT)
            m = jnp.maximum(jnp.max(s_past, axis=1, keepdims=True), jnp.max(s_own, axis=1, keepdims=True))
            p_past = jnp.exp2(s_past - m)
            p_own = jnp.exp2(s_own - m)
            denom = jnp.sum(p_past, axis=1, keepdims=True) + jnp.sum(p_own, axis=1, keepdims=True)
            o = _dot_nt(p_past.astype(_BF16), vbuf[buf, h].astype(_BF16)) + _dot(p_own.astype(_BF16), vn_ref[h])
            o_ref[h] = (o / denom).astype(o_ref.dtype)

    @pl.when(b == 0)
    def _():
        for cp in copies(0, 0):
            cp.start()

    for parity in range(2):
        @pl.when(b % 2 == parity)
        def _(parity=parity):
            @pl.when(b + 1 < n_steps)
            def _():
                for cp in copies(b + 1, 1 - parity):
                    cp.start()
            for cp in copies(b, parity):
                cp.wait()
            attend(parity)


def _sample_attend_call(page_table, sel, q_h, kn_h, vn_h, cache_kt, cache_vt, layer):
    bd, heads, s_len, head_dim = q_h.shape
    page = cache_kt.shape[-1]
    ppb = MOBA_BLOCK // page
    n_cols = s_len * MOBA_TOPK * MOBA_BLOCK
    body = functools.partial(_sample_attend_kernel, layer=layer, heads=heads, s_len=s_len, page=page, ppb=ppb)
    per_seq = pl.BlockSpec((None, heads, s_len, head_dim), lambda b, pt, sl: (b, 0, 0, 0))
    grid_spec = pltpu.PrefetchScalarGridSpec(
        num_scalar_prefetch=2,
        grid=(bd,),
        in_specs=[per_seq, per_seq, per_seq,
                  pl.BlockSpec(memory_space=pl.ANY), pl.BlockSpec(memory_space=pl.ANY)],
        out_specs=per_seq,
        scratch_shapes=[pltpu.VMEM((2, heads, head_dim, n_cols), _F32),
                        pltpu.VMEM((2, heads, head_dim, n_cols), _F32),
                        pltpu.SemaphoreType.DMA((2, 2))],
    )
    return pl.pallas_call(
        body,
        grid_spec=grid_spec,
        out_shape=jax.ShapeDtypeStruct((bd, heads, s_len, head_dim), _BF16),
        compiler_params=_cparams("arbitrary"),
        name="moba_sample",
    )(page_table, sel, q_h, kn_h, vn_h, cache_kt, cache_vt)


def _ssm_prep_kernel(lr_ref, li_ref, ldt_ref, brt_ref, bit_ref, cr_ref, ci_ref,
                     m_ref, ber_ref, bei_ref, cpr_ref, cpi_ref, aer_ref, aei_ref, *, tok_variants):
    tc = SSM_CHUNK
    hg = cr_ref.shape[0]
    dt = jnp.exp(ldt_ref[...])
    lr, li = lr_ref[...], li_ref[...]
    mag = jnp.exp(lr * dt)
    a_re, a_im = mag * jnp.cos(li * dt), mag * jnp.sin(li * dt)
    den = lr * lr + li * li
    f_re = ((a_re - 1.0) * lr + a_im * li) / den
    f_im = (a_im * lr - (a_re - 1.0) * li) / den
    brt, bit = brt_ref[...], bit_ref[...]
    bb_re = f_re * brt - f_im * bit
    bb_im = f_re * bit + f_im * brt
    cr, ci = cr_ref[...], ci_ref[...]

    pr, pi = jnp.ones_like(a_re), jnp.zeros_like(a_re)
    ca_re, ca_im, bp_re, bp_im, ap_re, ap_im = [], [], [], [], [], []
    for _ in range(tc + 1):
        ca_re.append(cr * pr - ci * pi)
        ca_im.append(cr * pi + ci * pr)
        bp_re.append(pr * bb_re - pi * bb_im)
        bp_im.append(pr * bb_im + pi * bb_re)
        ap_re.append(pr)
        ap_im.append(pi)
        pr, pi = pr * a_re - pi * a_im, pr * a_im + pi * a_re

    hi = lax.Precision.HIGHEST
    nt = (((1,), (1,)), ((), ()))
    lag_rows = lambda parts: jnp.concatenate(parts[:tc], axis=0)
    base = (lax.dot_general(bb_re, lag_rows(ca_re), nt, precision=hi, preferred_element_type=_F32)
            - lax.dot_general(bb_im, lag_rows(ca_im), nt, precision=hi, preferred_element_type=_F32))
    lane = lax.broadcasted_iota(jnp.int32, base.shape, 1)
    for s in range(tc):
        shifted = pltpu.roll(base, s * hg, 1) if s else base
        m_ref[s * hg:(s + 1) * hg, :] = jnp.where(lane >= s * hg, shifted, 0.0).astype(m_ref.dtype)

    zero = jnp.zeros_like(bb_re)
    for v, n in enumerate(tok_variants):
        ber_ref[v] = jnp.concatenate([bp_re[n - 1 - s] if s < n else zero for s in range(tc)],
                                     axis=0).astype(ber_ref.dtype)
        bei_ref[v] = jnp.concatenate([bp_im[n - 1 - s] if s < n else zero for s in range(tc)],
                                     axis=0).astype(bei_ref.dtype)
        aer_ref[v] = ap_re[n]
        aei_ref[v] = ap_im[n]
    cpr_ref[...] = jnp.concatenate(ca_re[1:], axis=0).astype(cpr_ref.dtype)
    cpi_ref[...] = (-jnp.concatenate(ca_im[1:], axis=0)).astype(cpi_ref.dtype)


def _ssm_prep_call(lam_re, lam_im, log_dt, b_re, b_im, c_re, c_im, *, tok_variants):
    g, p = lam_re.shape
    hg = b_re.shape[-1]
    w = SSM_CHUNK * hg
    nv = len(tok_variants)
    body = functools.partial(_ssm_prep_kernel, tok_variants=tok_variants)

    def full(*dims):
        return pl.BlockSpec((None,) + dims, lambda i: (i,) + (0,) * len(dims))

    return pl.pallas_call(
        body,
        grid=(g,),
        in_specs=[full(1, p), full(1, p), full(1, 1), full(hg, p), full(hg, p), full(hg, p), full(hg, p)],
        out_specs=[full(w, w), full(nv, w, p), full(nv, w, p), full(w, p), full(w, p), full(nv, 1, p), full(nv, 1, p)],
        out_shape=[jax.ShapeDtypeStruct((g, w, w), _BF16),
                   jax.ShapeDtypeStruct((g, nv, w, p), _BF16),
                   jax.ShapeDtypeStruct((g, nv, w, p), _BF16),
                   jax.ShapeDtypeStruct((g, w, p), _BF16),
                   jax.ShapeDtypeStruct((g, w, p), _BF16),
                   jax.ShapeDtypeStruct((g, nv, 1, p), _F32),
                   jax.ShapeDtypeStruct((g, nv, 1, p), _F32)],
        compiler_params=_cparams("parallel"),
        name="ssm_prep",
    )(lam_re[:, None, :], lam_im[:, None, :], log_dt[:, None, None],
      b_re.transpose(0, 2, 1), b_im.transpose(0, 2, 1), c_re, c_im)


def _ssm_tables(prep, d_skip, variant):
    m_intra, bend_re, bend_im, cp_re, cp_im, aend_re, aend_im = prep
    dvec = jnp.tile(d_skip[:, None, :], (1, SSM_CHUNK, 1)).reshape(d_skip.shape[0], 1, -1)
    return (m_intra,
            jnp.concatenate([bend_re[:, variant], bend_im[:, variant]], axis=-1),
            jnp.concatenate([cp_re, cp_im], axis=-1),
            jnp.concatenate([aend_re[:, variant], aend_im[:, variant]], axis=-1), dvec)


def _ssm_kernel(x_ref, mt_ref, be_ref, cp_ref, ae_ref, dv_ref, h0_ref, y_ref, hT_ref, s_ref, hs_ref,
                *, gb, nb, cps, nseg):
    p2 = ae_ref.shape[-1]
    half = p2 // 2
    r = nb * cps
    w = mt_ref.shape[-1]
    lo = lax.broadcasted_iota(jnp.int32, (1, 1, p2), 2) < half
    swap = lambda z: pltpu.roll(z, half, z.ndim - 1)

    def forms(z):
        zs = swap(z)
        return jnp.where(lo, z, zs), jnp.where(lo, -zs, z)

    def cmul(f, h):
        return f[0] * h + f[1] * swap(h)

    def cpow(z, n):
        out = None
        while n:
            if n & 1:
                out = z if out is None else cmul(forms(z), out)
            n >>= 1
            if n:
                z = cmul(forms(z), z)
        return out

    seg_rows = lambda sg: pl.ds(sg, r, stride=nseg)
    for g in range(gb):
        for sg in range(nseg):
            xb = x_ref[g, :, sg * w:(sg + 1) * w].astype(_BF16)
            s_ref[g, seg_rows(sg), :] = _dot(xb, be_ref[g])

    a = ae_ref[...]
    fa = forms(a)
    tile = lambda b, jj: pl.ds(pl.multiple_of((b * cps + jj) * nseg, nseg), nseg)
    zero = jnp.zeros((gb, nseg, p2), _F32)

    def local_step(jj, carry):
        nxt = []
        for b in range(nb):
            rows = tile(b, jj)
            hs_ref[:, rows, :] = carry[b]
            nxt.append(cmul(fa, carry[b]) + s_ref[:, rows, :])
        return tuple(nxt)

    seg_end = lax.fori_loop(0, cps, local_step, tuple(zero for _ in range(nb)))

    fs = forms(cpow(a, cps))
    seg_id = lax.broadcasted_iota(jnp.int32, zero.shape, 1)
    starts = []
    for b in range(nb):
        c = h0_ref[:, b:b + 1, :]
        st = zero
        for sg in range(nseg):
            st = jnp.where(seg_id == sg, c, st)
            c = cmul(fs, c) + seg_end[b][:, sg:sg + 1, :]
        starts.append(st)
        hT_ref[:, b:b + 1, :] = c

    def fix_step(jj, pw_):
        fp = forms(pw_)
        for b in range(nb):
            rows = tile(b, jj)
            hs_ref[:, rows, :] = hs_ref[:, rows, :] + cmul(fp, starts[b])
        return cmul(fa, pw_)

    one = jnp.where(lo, 1.0, 0.0).astype(_F32) + jnp.zeros_like(a)
    lax.fori_loop(0, cps, fix_step, one)

    for g in range(gb):
        for sg in range(nseg):
            x = x_ref[g, :, sg * w:(sg + 1) * w]
            y = _dot(x.astype(_BF16), mt_ref[g]) + _dot_nt(hs_ref[g, seg_rows(sg), :].astype(_BF16), cp_ref[g])
            y_ref[g, :, sg * w:(sg + 1) * w] = y + x * dv_ref[g]


def _ssm_call(xg, tables, h0, *, nb, cps, nseg, gb):
    m_intra, bend, cp, aend, dvec = tables
    g, r, _ = xg.shape
    w = m_intra.shape[-1]
    p2 = bend.shape[-1]
    assert r == nb * cps and xg.shape[-1] == nseg * w
    body = functools.partial(_ssm_kernel, gb=gb, nb=nb, cps=cps, nseg=nseg)

    def spec(*dims):
        return pl.BlockSpec((gb,) + dims, lambda i: (i,) + (0,) * len(dims))

    return pl.pallas_call(
        body,
        grid=(g // gb,),
        in_specs=[spec(r, nseg * w), spec(w, w), spec(w, p2), spec(w, p2), spec(1, p2), spec(1, w), spec(nb, p2)],
        out_specs=[spec(r, nseg * w), spec(nb, p2)],
        out_shape=[jax.ShapeDtypeStruct((g, r, nseg * w), _F32),
                   jax.ShapeDtypeStruct((g, nb, p2), _F32)],
        scratch_shapes=[pltpu.VMEM((gb, r * nseg, p2), _F32)] * 2,
        compiler_params=_cparams("parallel"),
        name="ssm",
    )(xg, m_intra, bend, cp, aend, dvec, h0)


def _merge_kernel(x_ref, oa_ref, y_ref, g_ref, wg_ref, wglu_ref, wpa_ref, wpb_ref, wout_ref, fg_ref,
                  o_ref, slab_ref, *, d_attn, d_ssm, hg, tok, final_norm):
    x = x_ref[...]
    d = x.shape[-1]
    hb = _rmsnorm(x, g_ref[...]).astype(_BF16)
    z_a = _dot(hb, wg_ref[:, 0:d_attn])
    ya = oa_ref[...].astype(_F32) * jax.nn.silu(z_a)
    br_a = _dot(ya.astype(_BF16), wpa_ref[...])
    ys = _chunks_to_tokens(y_ref, slab_ref, hg=hg, tok=tok)
    glu = _dot(jax.nn.gelu(ys).astype(_BF16), wglu_ref[...])
    z_b = _dot(hb, wg_ref[:, d_attn:d_attn + d_ssm])
    yb = glu[:, :d_ssm] * jax.nn.sigmoid(glu[:, d_ssm:]) * jax.nn.silu(z_b)
    br_b = _dot(yb.astype(_BF16), wpb_ref[...])
    g_a = _dot(hb, wg_ref[:, d_attn + d_ssm:d_attn + d_ssm + d])
    g_b = _dot(hb, wg_ref[:, d_attn + d_ssm + d:d_attn + d_ssm + 2 * d])
    merged = jax.nn.sigmoid(g_a) * br_a + jax.nn.sigmoid(g_b) * br_b
    out = x + _dot(merged.astype(_BF16), wout_ref[...])
    if final_norm:
        out = _rmsnorm(out, fg_ref[...])
    o_ref[...] = out


def _merge_call(x2d, oa, yg, g, w_g, w_glu, w_pa, w_pb, w_out, final_g, *, hg, tok, tm, final_norm, nseg=1):
    t, d = x2d.shape
    d_attn = oa.shape[1]
    n_groups = yg.shape[0]
    w = yg.shape[-1] // nseg
    d_ssm = n_groups * hg
    row = lambda i: (i, 0)
    const = lambda a: pl.BlockSpec(a.shape, lambda i: (0, 0), pipeline_mode=pl.Buffered(1))
    body = functools.partial(_merge_kernel, d_attn=d_attn, d_ssm=d_ssm, hg=hg, tok=tok, final_norm=final_norm)
    return pl.pallas_call(
        body,
        grid=(t // tm,),
        in_specs=[pl.BlockSpec((tm, d), row), pl.BlockSpec((tm, d_attn), row),
                  pl.BlockSpec((n_groups, tm // tok, w), lambda i: (0, i // nseg, i % nseg)),
                  const(g), const(w_g), const(w_glu), const(w_pa), const(w_pb), const(w_out), const(final_g)],
        out_specs=pl.BlockSpec((tm, d), row),
        out_shape=jax.ShapeDtypeStruct((t, d), _F32),
        scratch_shapes=[pltpu.VMEM((d_ssm // V7X_LANES, tm, V7X_LANES), _F32)],
        compiler_params=_cparams("parallel"),
        name="merge",
    )(x2d, oa, yg, g, w_g, w_glu, w_pa, w_pb, w_out, final_g)


def kernel(x_prompt, x_sample, cache_k, cache_v, state_ssm_re, state_ssm_im, page_table, norm_g, w_in,
           lam_re, lam_im, log_dt, b_re, b_im, c_re, c_im, d_skip, w_glu, w_pa, w_pb, w_out, final_norm_g):
    b, l, d = x_prompt.shape
    bd, s_len, _ = x_sample.shape
    depth = norm_g.shape[0]
    _, _, page, heads, head_dim = cache_k.shape
    n_groups, p_state = lam_re.shape[1:]
    hg = b_re.shape[-1]
    d_attn, d_ssm = heads * head_dim, n_groups * hg
    n_pages = page_table.shape[1]
    nbp = n_pages * page // MOBA_BLOCK
    assert w_in.shape[-1] == 4 * d_attn + 2 * d_ssm + 2 * d
    assert l % MOBA_BLOCK == 0 and MOBA_BLOCK % page == 0 and (n_pages * page) % MOBA_BLOCK == 0
    assert s_len <= SSM_CHUNK and SSM_CHUNK * hg == V7X_MXU_DIM and l % SSM_CHUNK == 0
    assert V7X_LANES % hg == 0 and d_ssm % V7X_LANES == 0 and d_attn % V7X_MXU_DIM == 0

    nseg = V7X_SUBLANES
    tm_p = l // nseg
    assert l % nseg == 0 and tm_p % MOBA_BLOCK == 0 and tm_p % (SSM_CHUNK * V7X_SUBLANES) == 0
    tm_s = bd * s_len
    xp = x_prompt.reshape(b * l, d)
    xs = x_sample.reshape(bd * s_len, d)
    cache_kt = cache_k.transpose(0, 1, 3, 4, 2)
    cache_vt = cache_v.transpose(0, 1, 3, 4, 2)
    zeros_state = jnp.zeros((n_groups, b, 2 * p_state), _F32)
    fg = final_norm_g[None, :]
    proj_kw = dict(d_attn=d_attn, d_ssm=d_ssm, head_dim=head_dim, hg=hg)

    kp, vp, ks, vs, hpr, hpi, hsr, hsi = ([] for _ in range(8))
    for layer in range(depth):
        wl = w_in[layer]
        o_z, o_u, o_zb, o_ga = 3 * d_attn, 4 * d_attn, 4 * d_attn + d_ssm, 4 * d_attn + 2 * d_ssm
        w_a = jnp.concatenate([wl[:, :o_z], wl[:, o_u:o_zb]], axis=1).astype(_BF16)
        w_g = jnp.concatenate([wl[:, o_z:o_u], wl[:, o_zb:o_ga], wl[:, o_ga:]], axis=1).astype(_BF16)
        gl = norm_g[layer][None, :]
        merge_w = (gl, w_g, w_glu[layer].astype(_BF16), w_pa[layer].astype(_BF16),
                   w_pb[layer].astype(_BF16), w_out[layer].astype(_BF16), fg)
        last = layer == depth - 1
        prep = _ssm_prep_call(lam_re[layer], lam_im[layer], log_dt[layer], b_re[layer], b_im[layer],
                              c_re[layer], c_im[layer], tok_variants=(SSM_CHUNK, s_len))

        qb, k, v, kb, vb, xg, kmean = _proj_call(xp, gl, w_a, tok=SSM_CHUNK, tm=tm_p, with_kmean=True, seq_len=l,
                                                 **proj_kw)
        oa, kmean_t = _moba_prompt_call(qb.reshape(b, l, d_attn), kb.reshape(b, l, d_attn), vb.reshape(b, l, d_attn),
                                        kmean.reshape(b, l // MOBA_BLOCK, d_attn), cache_kt, page_table, layer,
                                        head_dim=head_dim)
        yg, h_fin = _ssm_call(xg, _ssm_tables(prep, d_skip[layer], 0), zeros_state,
                              nb=b, cps=l // SSM_CHUNK // nseg, nseg=nseg, gb=4)
        xp = _merge_call(xp, oa.reshape(b * l, d_attn), yg, *merge_w, hg=hg, tok=SSM_CHUNK, tm=tm_p,
                         final_norm=last, nseg=nseg)
        kp.append(k.reshape(b, heads, head_dim, l).transpose(0, 3, 1, 2))
        vp.append(v.reshape(b, heads, head_dim, l).transpose(0, 3, 1, 2))
        hpr.append(h_fin[..., :p_state].transpose(1, 0, 2))
        hpi.append(h_fin[..., p_state:].transpose(1, 0, 2))

        qb, k, v, kb, vb, xg = _proj_call(xs, gl, w_a, tok=s_len, tm=tm_s, with_kmean=False, **proj_kw)
        sel = _select_call(qb.reshape(bd, s_len, d_attn), kmean_t.reshape(bd, d_attn, V7X_LANES),
                           heads=heads, head_dim=head_dim, nbp=nbp)
        sel = sel[:, :, :MOBA_TOPK].reshape(-1)
        to_heads = lambda a: a.reshape(bd, s_len, heads, head_dim).transpose(0, 2, 1, 3)
        oa = _sample_attend_call(page_table, sel, to_heads(qb), to_heads(kb), to_heads(vb),
                                 cache_kt, cache_vt, layer)
        oa = oa.transpose(0, 2, 1, 3).reshape(bd * s_len, d_attn)
        h0 = jnp.concatenate([state_ssm_re[layer], state_ssm_im[layer]], axis=-1).transpose(1, 0, 2)
        yg, h_fin = _ssm_call(xg, _ssm_tables(prep, d_skip[layer], 1), h0, nb=bd, cps=1, nseg=1, gb=4)
        xs = _merge_call(xs, oa, yg, *merge_w, hg=hg, tok=s_len, tm=tm_s, final_norm=last)
        ks.append(k.reshape(bd, s_len, heads, head_dim))
        vs.append(v.reshape(bd, s_len, heads, head_dim))
        hsr.append(h_fin[..., :p_state].transpose(1, 0, 2))
        hsi.append(h_fin[..., p_state:].transpose(1, 0, 2))

    return (xp.reshape(b, l, d), xs.reshape(bd, s_len, d),
            jnp.stack(kp), jnp.stack(vp), jnp.stack(ks), jnp.stack(vs),
            jnp.stack(hpr), jnp.stack(hpi), jnp.stack(hsr), jnp.stack(hsi))
```
